```python
import math, functools
import jax, jax.numpy as jnp
from jax import lax
import numpy as np

D_MODEL = 1024
BATCH = 4
SEQ = 4096
DEPTH = 4
DEC_BATCH = 128
DEC_SEQ = 4
PAST_LEN = 2048
PAGE_SIZE = 128

N_HEADS = 16
HEAD_DIM = D_MODEL // N_HEADS
HHD = N_HEADS * HEAD_DIM
D_FF = 2816
N_IDX_HEADS = 16
D_IDX = 64
DSA_TOPK = 256
MOBA_BLOCK = 256
MOBA_TOPK = 3
N_BUCKETS = 32
MAX_EXACT = 16
MAX_DISTANCE = 128
Q_BLOCK = 128
MOBA_Q_BLOCK = 32
N_MIXERS = 3
EPS = 1e-6
NEG_INF = float('-inf')
LAYER_KIND = tuple(i % N_MIXERS for i in range(DEPTH))
LAYER_SLOT = tuple(LAYER_KIND[:i].count(LAYER_KIND[i]) for i in range(DEPTH))
N_A_LAYERS = LAYER_KIND.count(0)
N_B_LAYERS = LAYER_KIND.count(1)
N_C_LAYERS = LAYER_KIND.count(2)
D_IN_A = 3 * HHD + N_IDX_HEADS * D_IDX + D_IDX + N_IDX_HEADS

kernel_name = 'hybrid_dsa_moba_stickbreak_decode_step'


def rms_norm(x, g):
    xf = x.astype(jnp.float32)
    y = xf * lax.rsqrt(jnp.mean(xf * xf, axis=-1, keepdims=True) + EPS)
    return (y * g.astype(jnp.float32)).astype(x.dtype)


def rel_bucket(dist):
    n = jnp.maximum(dist, 0)
    nf = jnp.maximum(n, 1).astype(jnp.float32)
    large = MAX_EXACT + (jnp.log(nf / MAX_EXACT) / math.log(MAX_DISTANCE / MAX_EXACT)
                         * (N_BUCKETS - MAX_EXACT)).astype(jnp.int32)
    large = jnp.minimum(large, N_BUCKETS - 1)
    return jnp.where(n < MAX_EXACT, n, large)


def swiglu_ffn(x, g, w_gu, w_down):
    a, b = jnp.split(rms_norm(x, g) @ w_gu, 2, axis=-1)
    return (jax.nn.silu(a) * b) @ w_down


def gather_rows(a, idx):
    return jax.vmap(lambda ai, ii: ai[ii])(a, idx)


def sweep_queries(fn, qb, q_pos, *qs):
    T = q_pos.shape[0]
    if T <= qb or T % qb:
        return fn(q_pos, *qs)
    nb = T // qb
    pos_b = q_pos.reshape(nb, qb)
    qs_b = tuple(jnp.moveaxis(a.reshape(a.shape[0], nb, qb, *a.shape[2:]), 1, 0) for a in qs)
    out = lax.map(lambda args: fn(*args), (pos_b, *qs_b))
    out = jnp.moveaxis(out, 0, 1)
    return out.reshape(out.shape[0], T, *out.shape[3:])


def join_past(new, past_rows, q_pos):
    if past_rows is None:
        return new
    return jnp.concatenate([past_rows, new.astype(past_rows.dtype)], axis=1)


def key_positions(q_pos, past_rows):
    if past_rows is None:
        return q_pos
    return jnp.concatenate([jnp.arange(past_rows.shape[1], dtype=jnp.int32), q_pos])


def dsa_mixer(h, q_pos, past, *, w_in, w_out, q_g, k_g, ki_g, rel_table):
    B, T, _ = h.shape
    o1 = HHD
    o2 = 2 * HHD
    o3 = 3 * HHD
    o4 = o3 + N_IDX_HEADS * D_IDX
    o5 = o4 + D_IDX
    q, k, v, qi, ki, wi = jnp.split(h @ w_in, [o1, o2, o3, o4, o5], axis=-1)
    q = rms_norm(q.reshape(B, T, N_HEADS, HEAD_DIM), q_g)
    k = rms_norm(k.reshape(B, T, N_HEADS, HEAD_DIM), k_g)
    v = v.reshape(B, T, N_HEADS, HEAD_DIM)
    qi = qi.reshape(B, T, N_IDX_HEADS, D_IDX)
    ki = rms_norm(ki, ki_g)
    wi = wi * (N_IDX_HEADS * D_IDX) ** -0.5
    pk, pv, pki = past if past is not None else (None, None, None)
    K = join_past(k, pk, q_pos)
    V = join_past(v, pv, q_pos)
    KI = join_past(ki, pki, q_pos)
    k_pos = key_positions(q_pos, pk)
    L = K.shape[1]
    topk = min(DSA_TOPK, L // 4)
    scale = HEAD_DIM ** -0.5

    def block(qp, qb, qib, wib):
        dots = jnp.einsum('bthd,bsd->bths', qib.astype(jnp.float32), KI.astype(jnp.float32))
        score = jnp.einsum('bth,bths->bts', wib.astype(jnp.float32), jax.nn.relu(dots))
        score = jnp.where(k_pos[None, None, :] <= qp[None, :, None], score, NEG_INF)
        _, idx = lax.top_k(score, topk)
        sel_pos = k_pos[idx]
        valid = sel_pos <= qp[None, :, None]
        ks = gather_rows(K, idx)
        vs = gather_rows(V, idx)
        bias = rel_table[rel_bucket(qp[None, :, None] - sel_pos)]
        logits = (jnp.einsum('bthd,btkhd->bthk', qb, ks).astype(jnp.float32) * scale
                  + jnp.moveaxis(bias, -1, 2).astype(jnp.float32))
        logits = jnp.where(valid[:, :, None, :], logits, NEG_INF)
        p = jax.nn.softmax(logits, axis=-1)
        return jnp.einsum('bthk,btkhd->bthd', p.astype(vs.dtype), vs)

    o = sweep_queries(block, Q_BLOCK, q_pos, q, qi, wi)
    return o.reshape(B, T, HHD) @ w_out, (k, v, ki)


def moba_mixer(h, q_pos, past, *, w_in, w_out, q_g, k_g, rel_table):
    B, T, _ = h.shape
    q, k, v = jnp.split(h @ w_in, 3, axis=-1)
    q = rms_norm(q.reshape(B, T, N_HEADS, HEAD_DIM), q_g)
    k = rms_norm(k.reshape(B, T, N_HEADS, HEAD_DIM), k_g)
    v = v.reshape(B, T, N_HEADS, HEAD_DIM)
    pk, pv = past if past is not None else (None, None)
    K = join_past(k, pk, q_pos)
    V = join_past(v, pv, q_pos)
    L = K.shape[1]
    L_pad = -(-L // MOBA_BLOCK) * MOBA_BLOCK
    nblk = L_pad // MOBA_BLOCK
    pad = ((0, 0), (0, L_pad - L), (0, 0), (0, 0))
    Kp = jnp.pad(K, pad)
    Vp = jnp.pad(V, pad)
    Kb = Kp.reshape(B, nblk, MOBA_BLOCK, N_HEADS, HEAD_DIM)
    Vb = Vp.reshape(B, nblk, MOBA_BLOCK, N_HEADS, HEAD_DIM)
    kmean = jnp.mean(Kb.astype(jnp.float32), axis=2)
    Kbh = jnp.transpose(Kb, (0, 3, 1, 2, 4))
    Vbh = jnp.transpose(Vb, (0, 3, 1, 2, 4))
    topb = min(MOBA_TOPK, nblk)
    blk_ids = jnp.arange(nblk, dtype=jnp.int32)
    offs = jnp.arange(MOBA_BLOCK, dtype=jnp.int32)
    head_ids = jnp.arange(N_HEADS, dtype=jnp.int32)
    table_h = rel_table.T
    scale = HEAD_DIM ** -0.5
    pick = jax.vmap(jax.vmap(lambda kb, s: kb[s]))

    def block(qp, qb):
        own_q = qp // MOBA_BLOCK
        own = own_q[0]
        gate = jnp.einsum('bthd,bnhd->bhtn', qb.astype(jnp.float32), kmean)
        past_mask = blk_ids[None, :] < own_q[:, None]
        gate = jnp.where(past_mask[None, None], gate, NEG_INF)
        _, sel = lax.top_k(gate, topb)
        sel_valid = sel < own_q[None, None, :, None]
        ks = pick(Kbh, sel)
        vs = pick(Vbh, sel)
        sel_pos = sel[..., None] * MOBA_BLOCK + offs
        bias_sel = table_h[head_ids[None, :, None, None, None],
                           rel_bucket(qp[None, None, :, None, None] - sel_pos)]
        lg_sel = jnp.einsum('bthd,bhtjsd->bhtjs', qb, ks).astype(jnp.float32) * scale + bias_sel
        lg_sel = jnp.where(sel_valid[..., None], lg_sel, NEG_INF)
        ko = lax.dynamic_slice_in_dim(Kp, own * MOBA_BLOCK, MOBA_BLOCK, axis=1)
        vo = lax.dynamic_slice_in_dim(Vp, own * MOBA_BLOCK, MOBA_BLOCK, axis=1)
        dist_o = qp[:, None] - (own * MOBA_BLOCK + offs)[None, :]
        bias_o = table_h[:, rel_bucket(dist_o)]
        lg_own = jnp.einsum('bthd,bshd->bhts', qb, ko).astype(jnp.float32) * scale + bias_o[None]
        lg_own = jnp.where((dist_o >= 0)[None, None], lg_own, NEG_INF)
        Bq, Hq, Tq = lg_own.shape[:3]
        n_sel = topb * MOBA_BLOCK
        logits = jnp.concatenate([lg_sel.reshape(Bq, Hq, Tq, n_sel), lg_own], axis=-1)
        p = jax.nn.softmax(logits, axis=-1)
        p_sel = p[..., :n_sel].reshape(lg_sel.shape)
        p_own = p[..., n_sel:]
        return (jnp.einsum('bhtjs,bhtjsd->bthd', p_sel.astype(vs.dtype), vs)
                + jnp.einsum('bhts,bshd->bthd', p_own.astype(vo.dtype), vo))

    o = sweep_queries(block, MOBA_Q_BLOCK, q_pos, q)
    return o.reshape(B, T, HHD) @ w_out, (k, v)


def stick_breaking_mixer(h, q_pos, past, *, w_in, w_out):
    B, T, _ = h.shape
    q, k, v = jnp.split(h @ w_in, 3, axis=-1)
    q = q.reshape(B, T, N_HEADS, HEAD_DIM)
    k = k.reshape(B, T, N_HEADS, HEAD_DIM)
    v = v.reshape(B, T, N_HEADS, HEAD_DIM)
    pk, pv = past if past is not None else (None, None)
    K = join_past(k, pk, q_pos)
    V = join_past(v, pv, q_pos)
    k_pos = key_positions(q_pos, pk)
    scale = HEAD_DIM ** -0.5

    def block(qp, qb):
        z = jnp.einsum('bthd,bshd->bhts', qb.astype(jnp.float32), K.astype(jnp.float32)) * scale
        strict = (k_pos[None, :] < qp[:, None])[None, None]
        log_rest = jnp.where(strict, -jax.nn.softplus(z), 0.0)
        tail = lax.cumsum(log_rest, axis=3, reverse=True) - log_rest
        a = jnp.where(strict, jnp.exp(jax.nn.log_sigmoid(z) + tail), 0.0)
        return jnp.einsum('bhts,bshd->bthd', a.astype(V.dtype), V)

    o = sweep_queries(block, Q_BLOCK, q_pos, q)
    return o.reshape(B, T, HHD) @ w_out, (k, v)


def macaron_layer(x, q_pos, past, mixer, g_mix, ffn1, ffn2):
    x = x + 0.5 * swiglu_ffn(x, *ffn1)
    o, rows = mixer(rms_norm(x, g_mix), q_pos, past)
    x = x + o
    x = x + 0.5 * swiglu_ffn(x, *ffn2)
    return x, rows


def setup_inputs(seed: int = 0) -> dict:
    key = jax.random.key(seed)
    ks = jax.random.split(key, 32)
    f32 = jnp.float32
    n_pages = PAST_LEN // PAGE_SIZE
    n_pool = (DEC_BATCH * n_pages * 5) // 4

    def nrm(k, shape, scale=1.0):
        return jax.random.normal(k, shape, f32) * scale

    def gain(k, shape):
        return 1.0 + 0.01 * jax.random.normal(k, shape, f32)

    perm = jax.random.permutation(ks[5], n_pool)
    page_table = perm[:DEC_BATCH * n_pages].reshape(DEC_BATCH, n_pages).astype(jnp.int32)
    return {
        'x_prompt': nrm(ks[0], (BATCH, SEQ, D_MODEL)),
        'x_sample': nrm(ks[1], (DEC_BATCH, DEC_SEQ, D_MODEL)),
        'cache_k': nrm(ks[2], (DEPTH, n_pool, PAGE_SIZE, N_HEADS, HEAD_DIM)),
        'cache_v': nrm(ks[3], (DEPTH, n_pool, PAGE_SIZE, N_HEADS, HEAD_DIM)),
        'cache_kidx': nrm(ks[4], (N_A_LAYERS, n_pool, PAGE_SIZE, D_IDX)),
        'page_table': page_table,
        'rel_table': nrm(ks[6], (N_BUCKETS, N_HEADS), 0.5),
        'norm_ffn1': gain(ks[7], (DEPTH, D_MODEL)),
        'ffn1_w_gu': nrm(ks[8], (DEPTH, D_MODEL, 2 * D_FF), D_MODEL ** -0.5),
        'ffn1_w_down': nrm(ks[9], (DEPTH, D_FF, D_MODEL), D_FF ** -0.5),
        'norm_mix': gain(ks[10], (DEPTH, D_MODEL)),
        'norm_ffn2': gain(ks[11], (DEPTH, D_MODEL)),
        'ffn2_w_gu': nrm(ks[12], (DEPTH, D_MODEL, 2 * D_FF), D_MODEL ** -0.5),
        'ffn2_w_down': nrm(ks[13], (DEPTH, D_FF, D_MODEL), D_FF ** -0.5),
        'a_w_in': nrm(ks[14], (N_A_LAYERS, D_MODEL, D_IN_A), D_MODEL ** -0.5),
        'a_w_out': nrm(ks[15], (N_A_LAYERS, HHD, D_MODEL), HHD ** -0.5),
        'a_q_norm': gain(ks[16], (N_A_LAYERS, HEAD_DIM)),
        'a_k_norm': gain(ks[17], (N_A_LAYERS, HEAD_DIM)),
        'a_kidx_norm': gain(ks[18], (N_A_LAYERS, D_IDX)),
        'b_w_in': nrm(ks[19], (N_B_LAYERS, D_MODEL, 3 * HHD), D_MODEL ** -0.5),
        'b_w_out': nrm(ks[20], (N_B_LAYERS, HHD, D_MODEL), HHD ** -0.5),
        'b_q_norm': gain(ks[21], (N_B_LAYERS, HEAD_DIM)),
        'b_k_norm': gain(ks[22], (N_B_LAYERS, HEAD_DIM)),
        'c_w_in': nrm(ks[23], (N_C_LAYERS, D_MODEL, 3 * HHD), D_MODEL ** -0.5),
        'c_w_out': nrm(ks[24], (N_C_LAYERS, HHD, D_MODEL), HHD ** -0.5),
    }


def reference(x_prompt, x_sample, cache_k, cache_v, cache_kidx, page_table, rel_table,
              norm_ffn1, ffn1_w_gu, ffn1_w_down, norm_mix, norm_ffn2, ffn2_w_gu, ffn2_w_down,
              a_w_in, a_w_out, a_q_norm, a_k_norm, a_kidx_norm,
              b_w_in, b_w_out, b_q_norm, b_k_norm, c_w_in, c_w_out):
    dec_b, n_pages = page_table.shape
    past_len = n_pages * PAGE_SIZE
    pos_p = jnp.arange(x_prompt.shape[1], dtype=jnp.int32)
    pos_s = past_len + jnp.arange(x_sample.shape[1], dtype=jnp.int32)

    def paged(pool, layer):
        rows = pool[layer, page_table]
        return rows.reshape(dec_b, past_len, *pool.shape[3:])

    new_k_p, new_v_p, new_ki_p = [], [], []
    new_k_s, new_v_s, new_ki_s = [], [], []
    for i in range(DEPTH):
        kind, slot = LAYER_KIND[i], LAYER_SLOT[i]
        if kind == 0:
            mixer = functools.partial(dsa_mixer, w_in=a_w_in[slot], w_out=a_w_out[slot],
                                      q_g=a_q_norm[slot], k_g=a_k_norm[slot],
                                      ki_g=a_kidx_norm[slot], rel_table=rel_table)
            past = (paged(cache_k, i), paged(cache_v, i), paged(cache_kidx, slot))
        elif kind == 1:
            mixer = functools.partial(moba_mixer, w_in=b_w_in[slot], w_out=b_w_out[slot],
                                      q_g=b_q_norm[slot], k_g=b_k_norm[slot], rel_table=rel_table)
            past = (paged(cache_k, i), paged(cache_v, i))
        else:
            mixer = functools.partial(stick_breaking_mixer, w_in=c_w_in[slot], w_out=c_w_out[slot])
            past = (paged(cache_k, i), paged(cache_v, i))
        ffn1 = (norm_ffn1[i], ffn1_w_gu[i], ffn1_w_down[i])
        ffn2 = (norm_ffn2[i], ffn2_w_gu[i], ffn2_w_down[i])
        x_prompt, rows_p = macaron_layer(x_prompt, pos_p, None, mixer, norm_mix[i], ffn1, ffn2)
        x_sample, rows_s = macaron_layer(x_sample, pos_s, past, mixer, norm_mix[i], ffn1, ffn2)
        new_k_p.append(rows_p[0])
        new_v_p.append(rows_p[1])
        new_k_s.append(rows_s[0])
        new_v_s.append(rows_s[1])
        if kind == 0:
            new_ki_p.append(rows_p[2])
            new_ki_s.append(rows_s[2])
    return (x_prompt, x_sample, jnp.stack(new_k_p), jnp.stack(new_v_p), jnp.stack(new_ki_p),
            jnp.stack(new_k_s), jnp.stack(new_v_s), jnp.stack(new_ki_s))
```

```python
import functools
import math

import jax
import jax.numpy as jnp
from jax import lax
from jax.experimental import pallas as pl
from jax.experimental.pallas import tpu as pltpu

F32 = jnp.float32
BF16 = jnp.bfloat16
I32 = jnp.int32

D_MODEL = 1024
N_HEADS = 16
HEAD_DIM = 64
HHD = N_HEADS * HEAD_DIM
LANES = 128
N_PAIRS = HHD // LANES
D_FF = 2816
FF_CHUNK = 256
N_IDX_HEADS = 16
D_IDX = 64
DSA_TOPK = 256
MOBA_BLOCK = 256
MOBA_TOPK = 3
N_BUCKETS = 32
MAX_EXACT = 16
MAX_DISTANCE = 128
PAGE_SIZE = 128
QB = 128
CH = 128
EPS = 1e-6
NEG = -1e30
INT_MIN = -(2 ** 31)
SCALE = HEAD_DIM ** -0.5
IDX_SCALE = (N_IDX_HEADS * D_IDX) ** -0.5
VMEM_LIMIT = 56 * 1024 * 1024
TOKEN_TILE = 512

KIND_DSA, KIND_MOBA, KIND_SB = 0, 1, 2


def _cparams(n_axes):
    return pltpu.CompilerParams(dimension_semantics=("parallel",) * n_axes,
                                vmem_limit_bytes=VMEM_LIMIT)


def _dot(a, b):
    return jnp.dot(a, b, preferred_element_type=F32)


def _dot_t(a, b):
    return lax.dot_general(a, b, (((1,), (1,)), ((), ())), preferred_element_type=F32)


def _split_hi_lo(v):
    hi = v.astype(BF16)
    lo = (v - hi.astype(F32)).astype(BF16)
    return hi, lo


def _rms(x, g):
    ms = jnp.mean(x * x, axis=-1, keepdims=True)
    return x * lax.rsqrt(ms + EPS) * g


def _softplus(z):
    return jnp.maximum(z, 0.0) + jnp.log1p(jnp.exp(-jnp.abs(z)))


def _ffn_body(x_ref, g_ref, wgu_ref, wd_ref, o_ref):
    x = x_ref[...]
    xn = _rms(x, g_ref[...]).astype(BF16)
    acc = jnp.zeros_like(x)
    for c in range(D_FF // FF_CHUNK):
        lo = c * FF_CHUNK
        a = _dot(xn, wgu_ref[:, lo:lo + FF_CHUNK])
        b = _dot(xn, wgu_ref[:, D_FF + lo:D_FF + lo + FF_CHUNK])
        h = (a * (1.0 / (1.0 + jnp.exp(-a))) * b).astype(BF16)
        acc = acc + _dot(h, wd_ref[lo:lo + FF_CHUNK, :])
    o_ref[...] = x + 0.5 * acc


def _ffn(x, g, wgu, wd):
    n = x.shape[0]
    tm = min(TOKEN_TILE, n)
    return pl.pallas_call(
        _ffn_body,
        grid=(n // tm,),
        in_specs=[pl.BlockSpec((tm, D_MODEL), lambda i: (i, 0)),
                  pl.BlockSpec((1, D_MODEL), lambda i: (0, 0)),
                  pl.BlockSpec((D_MODEL, 2 * D_FF), lambda i: (0, 0)),
                  pl.BlockSpec((D_FF, D_MODEL), lambda i: (0, 0))],
        out_specs=pl.BlockSpec((tm, D_MODEL), lambda i: (i, 0)),
        out_shape=jax.ShapeDtypeStruct(x.shape, F32),
        compiler_params=_cparams(1),
        name="ffn",
    )(x, g.reshape(1, D_MODEL), wgu, wd)


def _head_rms(y, g_row):
    r_i = lax.broadcasted_iota(I32, (HHD, LANES), 0) >> 6
    c_i = lax.broadcasted_iota(I32, (HHD, LANES), 1)
    gather = jnp.where(r_i == c_i, 1.0, 0.0).astype(BF16)
    r2 = lax.broadcasted_iota(I32, (LANES, HHD), 0)
    c2 = lax.broadcasted_iota(I32, (LANES, HHD), 1) >> 6
    spread = jnp.where(r2 == c2, 1.0, 0.0).astype(BF16)
    hi, lo = _split_hi_lo(y * y)
    ssq = _dot(hi, gather) + _dot(lo, gather)
    r = lax.rsqrt(ssq * (1.0 / HEAD_DIM) + EPS)
    rhi, rlo = _split_hi_lo(r)
    rfull = _dot(rhi, spread) + _dot(rlo, spread)
    return y * rfull * g_row


def _store_pairs(ref, y):
    for p in range(N_PAIRS):
        ref[p] = y[:, p * LANES:(p + 1) * LANES].astype(BF16)


def _proj_body(kind, with_kmean, *refs):
    if kind == KIND_DSA:
        (x_ref, g_ref, w_ref, qg_ref, kg_ref, wt_ref, kig_ref,
         k_out, v_out, q_pm, k_pm, v_pm, qi_pm, ki_out, ki2_out, wi_out) = refs
    elif kind == KIND_MOBA:
        x_ref, g_ref, w_ref, qg_ref, kg_ref, k_out, v_out, q_pm, k_pm, v_pm = refs[:10]
        kmean_out = refs[10] if with_kmean else None
    else:
        x_ref, g_ref, w_ref, k_out, v_out, q_pm, k_pm, v_pm = refs
    xn = _rms(x_ref[...], g_ref[...]).astype(BF16)
    q = _dot(xn, w_ref[:, 0:HHD])
    k = _dot(xn, w_ref[:, HHD:2 * HHD])
    v = _dot(xn, w_ref[:, 2 * HHD:3 * HHD])
    if kind != KIND_SB:
        q = _head_rms(q, qg_ref[...])
        k = _head_rms(k, kg_ref[...])
    k_out[...] = k
    v_out[...] = v
    _store_pairs(q_pm, q * SCALE)
    _store_pairs(k_pm, k)
    _store_pairs(v_pm, v)
    if kind == KIND_DSA:
        _store_pairs(qi_pm, _dot(xn, w_ref[:, 3 * HHD:4 * HHD]))
        ki2 = _dot(xn, wt_ref[:, 0:LANES])
        ki2 = ki2 * lax.rsqrt(jnp.mean(ki2 * ki2, axis=-1, keepdims=True) + EPS) * kig_ref[...]
        ki_out[...] = ki2[:, 0:D_IDX]
        ki2_out[...] = ki2.astype(BF16)
        wi_out[...] = _dot(xn, wt_ref[:, LANES:2 * LANES])[:, 0:N_IDX_HEADS] * IDX_SCALE
    if kind == KIND_MOBA and with_kmean:
        tm = k.shape[0]
        for j in range(tm // MOBA_BLOCK):
            kmean_out[j] = jnp.mean(k[j * MOBA_BLOCK:(j + 1) * MOBA_BLOCK], axis=0, keepdims=True)


def _proj(kind, x, g, w, qg=None, kg=None, w_tail=None, kig=None, with_kmean=False):
    n = x.shape[0]
    tm = min(TOKEN_TILE, n)
    row = lambda i: (i, 0)
    fixed = lambda i: (0, 0)
    pm_spec = pl.BlockSpec((N_PAIRS, tm, LANES), lambda i: (0, i, 0))
    pm_shape = jax.ShapeDtypeStruct((N_PAIRS, n, LANES), BF16)
    args = [x, g.reshape(1, D_MODEL), w]
    in_specs = [pl.BlockSpec((tm, D_MODEL), row), pl.BlockSpec((1, D_MODEL), fixed),
                pl.BlockSpec(w.shape, fixed)]
    if kind != KIND_SB:
        args += [jnp.tile(qg, N_HEADS).reshape(1, HHD), jnp.tile(kg, N_HEADS).reshape(1, HHD)]
        in_specs += [pl.BlockSpec((1, HHD), fixed)] * 2
    if kind == KIND_DSA:
        args += [w_tail, jnp.tile(kig, LANES // D_IDX).reshape(1, LANES)]
        in_specs += [pl.BlockSpec(w_tail.shape, fixed), pl.BlockSpec((1, LANES), fixed)]
    out_shape = [jax.ShapeDtypeStruct((n, HHD), F32)] * 2 + [pm_shape] * 3
    out_specs = [pl.BlockSpec((tm, HHD), row)] * 2 + [pm_spec] * 3
    if kind == KIND_DSA:
        out_shape += [pm_shape, jax.ShapeDtypeStruct((n, D_IDX), F32),
                      jax.ShapeDtypeStruct((n, LANES), BF16),
                      jax.ShapeDtypeStruct((n, N_IDX_HEADS), F32)]
        out_specs += [pm_spec, pl.BlockSpec((tm, D_IDX), row), pl.BlockSpec((tm, LANES), row),
                      pl.BlockSpec((tm, N_IDX_HEADS), row)]
    if kind == KIND_MOBA and with_kmean:
        out_shape += [jax.ShapeDtypeStruct((n // MOBA_BLOCK, 1, HHD), F32)]
        out_specs += [pl.BlockSpec((tm // MOBA_BLOCK, 1, HHD), lambda i: (i, 0, 0))]
    return pl.pallas_call(
        functools.partial(_proj_body, kind, with_kmean),
        grid=(n // tm,), in_specs=in_specs, out_specs=out_specs, out_shape=out_shape,
        compiler_params=_cparams(1), name="proj",
    )(*args)


def _outproj_body(x_ref, o_ref, w_ref, y_ref):
    o = jnp.concatenate([o_ref[p] for p in range(N_PAIRS)], axis=1)
    y_ref[...] = x_ref[...] + _dot(o, w_ref[...])


def _outproj(x, o_pm, w):
    n = x.shape[0]
    tm = min(TOKEN_TILE, n)
    return pl.pallas_call(
        _outproj_body,
        grid=(n // tm,),
        in_specs=[pl.BlockSpec((tm, D_MODEL), lambda i: (i, 0)),
                  pl.BlockSpec((N_PAIRS, tm, LANES), lambda i: (0, i, 0)),
                  pl.BlockSpec((HHD, D_MODEL), lambda i: (0, 0))],
        out_specs=pl.BlockSpec((tm, D_MODEL), lambda i: (i, 0)),
        out_shape=jax.ShapeDtypeStruct(x.shape, F32),
        compiler_params=_cparams(1), name="outproj",
    )(x, o_pm, w)


def _half_masks():
    lane = lax.broadcasted_iota(I32, (QB, LANES), 1)
    return lane < HEAD_DIM, lane >= HEAD_DIM


def _softmax_step(s, m, l, acc, v2):
    m_new = jnp.maximum(m, jnp.max(s, axis=1, keepdims=True))
    alpha = jnp.exp(m - m_new)
    p = jnp.exp(s - m_new)
    l = alpha * l + jnp.sum(p, axis=1, keepdims=True)
    acc = alpha * acc + _dot(p.astype(BF16), v2)
    return m_new, l, acc


def _sortable_key(score):
    bits = pltpu.bitcast(score + 0.0, I32)
    return bits ^ ((bits >> 31) & 0x7FFFFFFF)


def _topk_threshold(count_ge, rows, topk):
    zero = jnp.zeros((rows, 1), I32)
    t0 = jnp.where(count_ge(zero) >= topk, zero, jnp.full((rows, 1), INT_MIN, I32))

    def bit_step(i, t):
        cand = t + jnp.left_shift(jnp.int32(1), jnp.int32(30) - i)
        return jnp.where(count_ge(cand) >= topk, cand, t)

    return lax.fori_loop(0, 31, bit_step, t0)


def _top3_flags(gate):
    lane = lax.broadcasted_iota(I32, gate.shape, 1)
    sel = jnp.zeros(gate.shape, F32)
    for _ in range(MOBA_TOPK):
        mx = jnp.max(gate, axis=1, keepdims=True)
        cand = jnp.where((gate == mx) & (gate > -jnp.inf), lane, LANES)
        first = jnp.min(cand, axis=1, keepdims=True)
        pick = lane == first
        sel = jnp.where(pick, 1.0, sel)
        gate = jnp.where(pick, -jnp.inf, gate)
    return sel


def _prompt_attn_specs(batch, seq):
    nq = seq // QB
    q_spec = pl.BlockSpec((N_PAIRS, QB, LANES), lambda b, i: (0, b * nq + i, 0))
    kv_spec = pl.BlockSpec((N_PAIRS, seq, LANES), lambda b, i: (0, b, 0))
    bias_spec = pl.BlockSpec((N_HEADS, 3, QB, CH), lambda b, i: (0, 0, 0, 0))
    out_shape = jax.ShapeDtypeStruct((N_PAIRS, batch * seq, LANES), BF16)
    return nq, q_spec, kv_spec, bias_spec, out_shape


def _dsa_prompt_body(topk, q_ref, k_ref, v_ref, qi_ref, ki2_ref, wi_ref, bias_ref, o_ref,
                     sc_ref, key_ref, qim_ref, wib_ref):
    qb = pl.program_id(1)
    nck = qb + 1
    low, high = _half_masks()
    wi = wi_ref[...]
    for p in range(N_PAIRS):
        qi2 = qi_ref[p]
        qim_ref[2 * p] = jnp.where(low, qi2, jnp.zeros_like(qi2))
        qim_ref[2 * p + 1] = jnp.where(high, qi2, jnp.zeros_like(qi2))
    for h in range(N_HEADS):
        wib_ref[h] = jnp.broadcast_to(wi[:, h:h + 1], (QB, LANES))
    row = lax.broadcasted_iota(I32, (QB, CH), 0)
    col = lax.broadcasted_iota(I32, (QB, CH), 1)

    def score_chunk(c, carry):
        ki = ki2_ref[pl.ds(pl.multiple_of(c * CH, CH), CH), :]
        d = _dot_t(qim_ref[...].reshape(N_HEADS * QB, LANES), ki).reshape(N_HEADS, QB, CH)
        s = jnp.sum(wib_ref[...] * jnp.maximum(d, 0.0), axis=0)
        s = jnp.where(c * CH + col <= qb * QB + row, s, -jnp.inf)
        key_ref[c] = _sortable_key(s)
        return carry

    lax.fori_loop(0, nck, score_chunk, 0)

    def count_ge(cand):
        candb = jnp.broadcast_to(cand, (QB, CH))

        def body(c, acc):
            return acc + jnp.where(key_ref[c] >= candb, 1.0, 0.0)

        acc = lax.fori_loop(0, nck, body, jnp.zeros((QB, CH), F32))
        return jnp.sum(acc, axis=1, keepdims=True)

    thr = jnp.broadcast_to(_topk_threshold(count_ge, QB, float(topk)), (QB, CH))

    def mask_chunk(c, carry):
        sc_ref[c] = jnp.where(key_ref[c] >= thr, 0.0, NEG)
        return carry

    lax.fori_loop(0, nck, mask_chunk, 0)

    def pair_body(p, carry):
        q2 = q_ref[p]
        outs = []
        for half, hm in enumerate((low, high)):
            qm = jnp.where(hm, q2, jnp.zeros_like(q2))
            h = 2 * p + half

            def chunk(c, st):
                start = pl.multiple_of(c * CH, CH)
                s = _dot_t(qm, k_ref[p, pl.ds(start, CH), :])
                s = s + bias_ref[h, jnp.minimum(qb - c, 2)] + sc_ref[c]
                return _softmax_step(s, *st, v_ref[p, pl.ds(start, CH), :])

            m0 = jnp.full((QB, 1), NEG, F32)
            _, l, acc = lax.fori_loop(0, nck, chunk, (m0, jnp.zeros((QB, 1), F32),
                                                      jnp.zeros((QB, LANES), F32)))
            outs.append(acc / l)
        o_ref[p] = jnp.where(low, outs[0], outs[1]).astype(BF16)
        return carry

    lax.fori_loop(0, N_PAIRS, pair_body, 0)


def _dsa_prompt(batch, seq, q_pm, k_pm, v_pm, qi_pm, ki2, wi, bias):
    nq, q_spec, kv_spec, bias_spec, out_shape = _prompt_attn_specs(batch, seq)
    topk = min(DSA_TOPK, seq // 4)
    return pl.pallas_call(
        functools.partial(_dsa_prompt_body, topk),
        grid=(batch, nq),
        in_specs=[q_spec, kv_spec, kv_spec, q_spec,
                  pl.BlockSpec((seq, LANES), lambda b, i: (b, 0)),
                  pl.BlockSpec((QB, N_IDX_HEADS), lambda b, i: (b * nq + i, 0)),
                  bias_spec],
        out_specs=q_spec, out_shape=out_shape,
        scratch_shapes=[pltpu.VMEM((nq, QB, CH), F32), pltpu.VMEM((nq, QB, CH), I32),
                        pltpu.VMEM((N_HEADS, QB, LANES), BF16), pltpu.VMEM((N_HEADS, QB, LANES), F32)],
        compiler_params=_cparams(2), name="dsa_prompt",
    )(q_pm, k_pm, v_pm, qi_pm, ki2, wi, bias)


def _moba_prompt_body(q_ref, k_ref, v_ref, km_ref, bias_ref, o_ref):
    qb = pl.program_id(1)
    nck = qb + 1
    own = qb // (MOBA_BLOCK // QB)
    low, high = _half_masks()
    lane = lax.broadcasted_iota(I32, (QB, LANES), 1)
    rowi = lax.broadcasted_iota(I32, (LANES, CH), 0)

    def pair_body(p, carry):
        q2 = q_ref[p]
        km2 = km_ref[p]
        outs = []
        for half, hm in enumerate((low, high)):
            qm = jnp.where(hm, q2, jnp.zeros_like(q2))
            h = 2 * p + half
            gate = jnp.where(lane < own, _dot_t(qm, km2), -jnp.inf)
            sel = _top3_flags(gate).astype(BF16)

            def chunk(c, st):
                start = pl.multiple_of(c * CH, CH)
                blk = c // (MOBA_BLOCK // CH)
                onehot = jnp.where(rowi == blk, 1.0, 0.0).astype(BF16)
                vis = _dot(sel, onehot) + jnp.where(blk == own, 1.0, 0.0)
                s = _dot_t(qm, k_ref[p, pl.ds(start, CH), :])
                s = s + bias_ref[h, jnp.minimum(qb - c, 2)] + jnp.where(vis > 0.5, 0.0, NEG)
                return _softmax_step(s, *st, v_ref[p, pl.ds(start, CH), :])

            m0 = jnp.full((QB, 1), NEG, F32)
            _, l, acc = lax.fori_loop(0, nck, chunk, (m0, jnp.zeros((QB, 1), F32),
                                                      jnp.zeros((QB, LANES), F32)))
            outs.append(acc / l)
        o_ref[p] = jnp.where(low, outs[0], outs[1]).astype(BF16)
        return carry

    lax.fori_loop(0, N_PAIRS, pair_body, 0)


def _moba_prompt(batch, seq, q_pm, k_pm, v_pm, kmean_pm, bias):
    nq, q_spec, kv_spec, bias_spec, out_shape = _prompt_attn_specs(batch, seq)
    return pl.pallas_call(
        _moba_prompt_body,
        grid=(batch, nq),
        in_specs=[q_spec, kv_spec, kv_spec,
                  pl.BlockSpec((N_PAIRS, None, LANES, LANES), lambda b, i: (0, b, 0, 0)),
                  bias_spec],
        out_specs=q_spec, out_shape=out_shape,
        compiler_params=_cparams(2), name="moba_prompt",
    )(q_pm, k_pm, v_pm, kmean_pm, bias)


def _revcum_matrix():
    j = lax.broadcasted_iota(I32, (CH, 2 * CH), 0)
    s = lax.broadcasted_iota(I32, (CH, 2 * CH), 1)
    return jnp.where((j > s) | (s >= CH), 1.0, 0.0).astype(BF16)


def _sb_chunk(z, strict, tail_in, v2, ue):
    sp = _softplus(z)
    lr = jnp.where(strict, -sp, 0.0)
    hi, lo = _split_hi_lo(lr)
    cs = _dot(hi, ue) + _dot(lo, ue)
    a = jnp.where(strict, jnp.exp(z - sp + tail_in + cs[:, :CH]), 0.0)
    return _dot(a.astype(BF16), v2), tail_in + cs[:, CH:]


def _sb_prompt_body(q_ref, k_ref, v_ref, o_ref):
    qb = pl.program_id(1)
    nck = qb + 1
    low, high = _half_masks()
    row = lax.broadcasted_iota(I32, (QB, CH), 0)
    col = lax.broadcasted_iota(I32, (QB, CH), 1)
    ue = _revcum_matrix()

    def pair_body(p, carry):
        q2 = q_ref[p]
        outs = []
        for hm in (low, high):
            qm = jnp.where(hm, q2, jnp.zeros_like(q2))

            def chunk(i, st):
                acc, tail = st
                c = qb - i
                start = pl.multiple_of(c * CH, CH)
                z = _dot_t(qm, k_ref[p, pl.ds(start, CH), :])
                strict = c * CH + col < qb * QB + row
                d_acc, tail = _sb_chunk(z, strict, tail, v_ref[p, pl.ds(start, CH), :], ue)
                return acc + d_acc, tail

            acc, _ = lax.fori_loop(0, nck, chunk, (jnp.zeros((QB, LANES), F32),
                                                   jnp.zeros((QB, CH), F32)))
            outs.append(acc)
        o_ref[p] = jnp.where(low, outs[0], outs[1]).astype(BF16)
        return carry

    lax.fori_loop(0, N_PAIRS, pair_body, 0)


def _sb_prompt(batch, seq, q_pm, k_pm, v_pm):
    nq, q_spec, kv_spec, _, out_shape = _prompt_attn_specs(batch, seq)
    return pl.pallas_call(
        _sb_prompt_body,
        grid=(batch, nq),
        in_specs=[q_spec, kv_spec, kv_spec],
        out_specs=q_spec, out_shape=out_shape,
        compiler_params=_cparams(2), name="sb_prompt",
    )(q_pm, k_pm, v_pm)


N_ROWS = 64


def _page_specs(n_pages, layer, width):
    return [pl.BlockSpec((None, None, PAGE_SIZE, width),
                         functools.partial(lambda b, pt, pg: (layer, pt[b, pg], 0, 0), pg=pg))
            for pg in range(n_pages)]


def _seq_spec(shape):
    return pl.BlockSpec((None,) + tuple(shape), lambda b, pt: (b,) + (0,) * len(shape))


def _dec_softmax_chunks(qbd, chunks):
    m = jnp.full((N_ROWS, 1), NEG, F32)
    l = jnp.zeros((N_ROWS, 1), F32)
    acc = jnp.zeros((N_ROWS, HHD), F32)
    for kc, vc, add in chunks:
        s = _dot_t(qbd, kc) + add
        m, l, acc = _softmax_step(s, m, l, acc, vc)
    return acc / l


def _dec_bias(bias_ref, n_pages, pg):
    if pg == n_pages:
        return bias_ref[2]
    return bias_ref[1] if pg == n_pages - 1 else bias_ref[0]


def _dsa_sample_body(n_pages, dec_seq, topk, pt_ref, qbd_ref, qi_ref, wib_ref, *refs):
    k_pages = refs[0:n_pages]
    v_pages = refs[n_pages:2 * n_pages]
    ki_pages = refs[2 * n_pages:3 * n_pages]
    knew_ref, vnew_ref, kinew_ref, bias_ref, o_ref = refs[3 * n_pages:]
    qi = qi_ref[...]
    wib = wib_ref[...]
    rows8 = lax.broadcasted_iota(I32, (8, CH), 0)
    lane8 = lax.broadcasted_iota(I32, (8, CH), 1)
    keys = []
    for pg in range(n_pages + 1):
        ki = ki_pages[pg][...].astype(BF16) if pg < n_pages else kinew_ref[...]
        wr = wib * jnp.maximum(_dot_t(qi, ki), 0.0)
        per_t = [jnp.sum(wr[t * N_HEADS:(t + 1) * N_HEADS], axis=0, keepdims=True)
                 for t in range(dec_seq)]
        sc = jnp.concatenate(per_t + [jnp.zeros((8 - dec_seq, CH), F32)], axis=0)
        valid = rows8 < dec_seq
        if pg == n_pages:
            valid = valid & (lane8 <= rows8)
        keys.append(_sortable_key(jnp.where(valid, sc, -jnp.inf)))

    def count_ge(cand):
        candb = jnp.broadcast_to(cand, (8, CH))
        acc = jnp.zeros((8, CH), F32)
        for kk in keys:
            acc = acc + jnp.where(kk >= candb, 1.0, 0.0)
        return jnp.sum(acc, axis=1, keepdims=True)

    thr = jnp.broadcast_to(_topk_threshold(count_ge, 8, float(topk)), (8, CH))
    qbd = qbd_ref[...]
    chunks = []
    for pg in range(n_pages + 1):
        mb = jnp.where(keys[pg] >= thr, 0.0, NEG)
        mb64 = jnp.concatenate([jnp.broadcast_to(mb[t:t + 1], (N_HEADS, CH))
                                for t in range(dec_seq)], axis=0)
        kc = k_pages[pg][...].astype(BF16) if pg < n_pages else knew_ref[...]
        vc = v_pages[pg][...].astype(BF16) if pg < n_pages else vnew_ref[...]
        chunks.append((kc, vc, mb64 + _dec_bias(bias_ref, n_pages, pg)))
    o_ref[...] = _dec_softmax_chunks(qbd, chunks)


def _moba_sample_body(n_pages, pt_ref, qbd_ref, *refs):
    k_pages = refs[0:n_pages]
    v_pages = refs[n_pages:2 * n_pages]
    knew_ref, vnew_ref, bias_ref, o_ref = refs[2 * n_pages:]
    per_blk = MOBA_BLOCK // PAGE_SIZE
    n_blk = n_pages // per_blk
    qbd = qbd_ref[...]
    means = []
    for n in range(n_blk):
        tot = jnp.zeros((1, HHD), F32)
        for pg in range(n * per_blk, (n + 1) * per_blk):
            tot = tot + jnp.sum(k_pages[pg][...], axis=0, keepdims=True)
        means.append(tot * (1.0 / MOBA_BLOCK))
    kmean = jnp.concatenate(means + [jnp.zeros((LANES - n_blk, HHD), F32)], axis=0).astype(BF16)
    lane = lax.broadcasted_iota(I32, (N_ROWS, LANES), 1)
    gate = jnp.where(lane < n_blk, _dot_t(qbd, kmean), -jnp.inf)
    sel = _top3_flags(gate)
    chunks = []
    for pg in range(n_pages + 1):
        if pg < n_pages:
            n = pg // per_blk
            add = jnp.where(sel[:, n:n + 1] > 0.5, 0.0, NEG) + _dec_bias(bias_ref, n_pages, pg)
            chunks.append((k_pages[pg][...].astype(BF16), v_pages[pg][...].astype(BF16), add))
        else:
            chunks.append((knew_ref[...], vnew_ref[...], _dec_bias(bias_ref, n_pages, pg)))
    o_ref[...] = _dec_softmax_chunks(qbd, chunks)


def _sb_sample_body(n_pages, dec_seq, pt_ref, qbd_ref, *refs):
    k_pages = refs[0:n_pages]
    v_pages = refs[n_pages:2 * n_pages]
    knew_ref, vnew_ref, o_ref = refs[2 * n_pages:]
    qbd = qbd_ref[...]
    ue = _revcum_matrix()
    tok = lax.broadcasted_iota(I32, (N_ROWS, CH), 0) // N_HEADS
    col = lax.broadcasted_iota(I32, (N_ROWS, CH), 1)
    acc = jnp.zeros((N_ROWS, HHD), F32)
    tail = jnp.zeros((N_ROWS, CH), F32)
    for pg in range(n_pages, -1, -1):
        if pg == n_pages:
            kc, vc, strict = knew_ref[...], vnew_ref[...], col < tok
        else:
            kc, vc = k_pages[pg][...].astype(BF16), v_pages[pg][...].astype(BF16)
            strict = col >= 0
        d_acc, tail = _sb_chunk(_dot_t(qbd, kc), strict, tail, vc, ue)
        acc = acc + d_acc
    o_ref[...] = acc


def _sample_attn(kind, layer, page_table, cache_k4, cache_v4, qbd, knew, vnew, bias=None,
                 dsa=None):
    n_seq, n_pages = page_table.shape
    dec_seq = N_ROWS // N_HEADS
    args = [qbd]
    in_specs = [_seq_spec((N_ROWS, HHD))]
    if kind == KIND_DSA:
        qi_r, wib, cache_ki4, slot, kinew = dsa
        args += [qi_r, wib]
        in_specs += [_seq_spec((N_ROWS, D_IDX)), _seq_spec((N_ROWS, CH))]
    args += [cache_k4] * n_pages + [cache_v4] * n_pages
    in_specs += _page_specs(n_pages, layer, HHD) * 2
    if kind == KIND_DSA:
        args += [cache_ki4] * n_pages
        in_specs += _page_specs(n_pages, slot, D_IDX)
    args += [knew, vnew]
    in_specs += [_seq_spec((CH, HHD))] * 2
    if kind == KIND_DSA:
        args += [kinew]
        in_specs += [_seq_spec((CH, D_IDX))]
    if kind != KIND_SB:
        args += [bias]
        in_specs += [pl.BlockSpec((3, N_ROWS, CH), lambda b, pt: (0, 0, 0))]
    if kind == KIND_DSA:
        topk = min(DSA_TOPK, (n_pages * PAGE_SIZE + dec_seq) // 4)
        body = functools.partial(_dsa_sample_body, n_pages, dec_seq, topk)
    elif kind == KIND_MOBA:
        body = functools.partial(_moba_sample_body, n_pages)
    else:
        body = functools.partial(_sb_sample_body, n_pages, dec_seq)
    return pl.pallas_call(
        body,
        grid_spec=pltpu.PrefetchScalarGridSpec(
            num_scalar_prefetch=1, grid=(n_seq,), in_specs=in_specs,
            out_specs=_seq_spec((N_ROWS, HHD))),
        out_shape=jax.ShapeDtypeStruct((n_seq, N_ROWS, HHD), F32),
        compiler_params=_cparams(1), name="sample_attn",
    )(page_table, *args)


def _rel_bucket(dist):
    n = jnp.maximum(dist, 0)
    nf = jnp.maximum(n, 1).astype(F32)
    large = MAX_EXACT + (jnp.log(nf / MAX_EXACT) / math.log(MAX_DISTANCE / MAX_EXACT)
                         * (N_BUCKETS - MAX_EXACT)).astype(I32)
    large = jnp.minimum(large, N_BUCKETS - 1)
    return jnp.where(n < MAX_EXACT, n, large)


def _bias_lookup(rel_table, dist):
    b = jnp.moveaxis(rel_table[_rel_bucket(dist)], -1, 0)
    return jnp.where(dist[None] >= 0, b, NEG)


def _prompt_bias_tiles(rel_table):
    i = jnp.arange(QB, dtype=I32)[:, None]
    j = jnp.arange(CH, dtype=I32)[None, :]
    tiles = [_bias_lookup(rel_table, u * CH + i - j) for u in range(3)]
    return jnp.stack(tiles, axis=1).astype(F32)


def _sample_bias_tiles(rel_table, dec_seq):
    t = jnp.arange(dec_seq, dtype=I32)[:, None]
    j = jnp.arange(CH, dtype=I32)[None, :]
    far = _bias_lookup(rel_table, jnp.full((dec_seq, CH), 2 * PAGE_SIZE, I32))
    last = _bias_lookup(rel_table, PAGE_SIZE + t - j)
    new = _bias_lookup(rel_table, jnp.where(j < dec_seq, t - j, -1))
    tiles = jnp.stack([far, last, new], axis=0)
    return jnp.transpose(tiles, (0, 2, 1, 3)).reshape(3, dec_seq * N_HEADS, CH).astype(F32)


def _from_pairs(y_pm):
    return jnp.transpose(y_pm, (1, 0, 2)).reshape(y_pm.shape[1], HHD)


def _to_pairs(y):
    return jnp.transpose(y.reshape(y.shape[0], N_PAIRS, LANES), (1, 0, 2))


def _block_diag_rows(q, n_seq, dec_seq):
    q5 = q.reshape(n_seq, dec_seq, 1, N_HEADS, HEAD_DIM)
    eye = jnp.eye(N_HEADS, dtype=q.dtype).reshape(1, 1, N_HEADS, N_HEADS, 1)
    return (q5 * eye).reshape(n_seq, dec_seq * N_HEADS, HHD)


def _diag_heads(o_bd, n_seq, dec_seq):
    o6 = o_bd.reshape(n_seq, dec_seq, N_HEADS, N_HEADS, HEAD_DIM)
    hh = jnp.arange(N_HEADS)
    return o6[:, :, hh, hh, :].reshape(n_seq * dec_seq, HHD)


def _pad_new(y, n_seq, dec_seq):
    y3 = y.reshape(n_seq, dec_seq, y.shape[-1])
    return jnp.pad(y3, ((0, 0), (0, CH - dec_seq), (0, 0)))


def kernel(x_prompt, x_sample, cache_k, cache_v, cache_kidx, page_table, rel_table,
           norm_ffn1, ffn1_w_gu, ffn1_w_down, norm_mix, norm_ffn2, ffn2_w_gu, ffn2_w_down,
           a_w_in, a_w_out, a_q_norm, a_k_norm, a_kidx_norm,
           b_w_in, b_w_out, b_q_norm, b_k_norm, c_w_in, c_w_out):
    batch, seq, _ = x_prompt.shape
    n_seq, dec_seq, _ = x_sample.shape
    depth = cache_k.shape[0]
    n_pool = cache_k.shape[1]
    n_pages = page_table.shape[1]
    assert seq % TOKEN_TILE == 0 and seq % MOBA_BLOCK == 0
    assert dec_seq * N_HEADS == N_ROWS and n_pages % (MOBA_BLOCK // PAGE_SIZE) == 0
    assert cache_k.shape[2] == PAGE_SIZE

    xp = x_prompt.reshape(batch * seq, D_MODEL)
    xs = x_sample.reshape(n_seq * dec_seq, D_MODEL)
    cache_k4 = cache_k.reshape(depth, n_pool, PAGE_SIZE, HHD)
    cache_v4 = cache_v.reshape(depth, n_pool, PAGE_SIZE, HHD)
    bias_p = _prompt_bias_tiles(rel_table)
    bias_s = _sample_bias_tiles(rel_table, dec_seq)
    n_blk = seq // MOBA_BLOCK

    new_k_p, new_v_p, new_ki_p, new_k_s, new_v_s, new_ki_s = [], [], [], [], [], []
    counts = [0, 0, 0]
    for i in range(depth):
        kind = i % 3
        slot = counts[kind]
        counts[kind] += 1
        wgu1, wd1 = ffn1_w_gu[i].astype(BF16), ffn1_w_down[i].astype(BF16)
        wgu2, wd2 = ffn2_w_gu[i].astype(BF16), ffn2_w_down[i].astype(BF16)
        xp = _ffn(xp, norm_ffn1[i], wgu1, wd1)
        xs = _ffn(xs, norm_ffn1[i], wgu1, wd1)

        if kind == KIND_DSA:
            w_in = a_w_in[slot]
            w_main = w_in[:, :4 * HHD].astype(BF16)
            w_ki = w_in[:, 4 * HHD:4 * HHD + D_IDX]
            w_wi = jnp.pad(w_in[:, 4 * HHD + D_IDX:], ((0, 0), (0, LANES - N_IDX_HEADS)))
            w_tail = jnp.concatenate([w_ki, w_ki, w_wi], axis=1).astype(BF16)
            kw = dict(qg=a_q_norm[slot], kg=a_k_norm[slot], w_tail=w_tail, kig=a_kidx_norm[slot])
            kp, vp, qp_pm, kp_pm, vp_pm, qip_pm, kip, ki2p, wip = _proj(kind, xp, norm_mix[i], w_main, **kw)
            ks, vs, qs_pm, ks_pm, vs_pm, qis_pm, kis, ki2s, wis = _proj(kind, xs, norm_mix[i], w_main, **kw)
            op_pm = _dsa_prompt(batch, seq, qp_pm, kp_pm, vp_pm, qip_pm, ki2p, wip, bias_p)
            qi_r = _from_pairs(qis_pm).reshape(n_seq, N_ROWS, D_IDX)
            wib = jnp.broadcast_to(wis.reshape(n_seq, N_ROWS, 1), (n_seq, N_ROWS, CH))
            cache_ki4 = cache_kidx
            dsa = (qi_r, wib, cache_ki4, slot, _pad_new(ki2s[:, :D_IDX], n_seq, dec_seq))
            w_out = a_w_out[slot]
            new_ki_p.append(kip.reshape(batch, seq, D_IDX))
            new_ki_s.append(kis.reshape(n_seq, dec_seq, D_IDX))
        elif kind == KIND_MOBA:
            kw = dict(qg=b_q_norm[slot], kg=b_k_norm[slot])
            w_in = b_w_in[slot].astype(BF16)
            kp, vp, qp_pm, kp_pm, vp_pm, kmean = _proj(kind, xp, norm_mix[i], w_in, with_kmean=True, **kw)
            ks, vs, qs_pm, ks_pm, vs_pm = _proj(kind, xs, norm_mix[i], w_in, **kw)
            km = jnp.transpose(kmean.reshape(batch, n_blk, N_PAIRS, LANES), (2, 0, 1, 3))
            km = jnp.pad(km, ((0, 0), (0, 0), (0, LANES - n_blk), (0, 0))).astype(BF16)
            op_pm = _moba_prompt(batch, seq, qp_pm, kp_pm, vp_pm, km, bias_p)
            dsa = None
            w_out = b_w_out[slot]
        else:
            w_in = c_w_in[slot].astype(BF16)
            kp, vp, qp_pm, kp_pm, vp_pm = _proj(kind, xp, norm_mix[i], w_in)
            ks, vs, qs_pm, ks_pm, vs_pm = _proj(kind, xs, norm_mix[i], w_in)
            op_pm = _sb_prompt(batch, seq, qp_pm, kp_pm, vp_pm)
            dsa = None
            w_out = c_w_out[slot]

        qbd = _block_diag_rows(_from_pairs(qs_pm), n_seq, dec_seq)
        knew = _pad_new(_from_pairs(ks_pm), n_seq, dec_seq)
        vnew = _pad_new(_from_pairs(vs_pm), n_seq, dec_seq)
        o_bd = _sample_attn(kind, i, page_table, cache_k4, cache_v4, qbd, knew, vnew,
                            bias=bias_s, dsa=dsa)
        os_pm = _to_pairs(_diag_heads(o_bd, n_seq, dec_seq)).astype(BF16)

        w_out = w_out.astype(BF16)
        xp = _outproj(xp, op_pm, w_out)
        xs = _outproj(xs, os_pm, w_out)
        xp = _ffn(xp, norm_ffn2[i], wgu2, wd2)
        xs = _ffn(xs, norm_ffn2[i], wgu2, wd2)

        new_k_p.append(kp.reshape(batch, seq, N_HEADS, HEAD_DIM))
        new_v_p.append(vp.reshape(batch, seq, N_HEADS, HEAD_DIM))
        new_k_s.append(ks.reshape(n_seq, dec_seq, N_HEADS, HEAD_DIM))
        new_v_s.append(vs.reshape(n_seq, dec_seq, N_HEADS, HEAD_DIM))

    return (xp.reshape(batch, seq, D_MODEL), xs.reshape(n_seq, dec_seq, D_MODEL),
            jnp.stack(new_k_p), jnp.stack(new_v_p), jnp.stack(new_ki_p),
            jnp.stack(new_k_s), jnp.stack(new_v_s), jnp.stack(new_ki_s))
```

```python
import functools
import math

import jax
import jax.numpy as jnp
from jax import lax
from jax.experimental import pallas as pl
from jax.experimental.pallas import tpu as pltpu

F32 = jnp.float32
BF16 = jnp.bfloat16
I32 = jnp.int32

D_MODEL = 1024
N_HEADS = 16
HEAD_DIM = 64
HHD = N_HEADS * HEAD_DIM
LANES = 128
N_PAIRS = HHD // LANES
D_FF = 2816
FF_CHUNK = 256
N_IDX_HEADS = 16
D_IDX = 64
DSA_TOPK = 256
MOBA_BLOCK = 256
MOBA_TOPK = 3
N_BUCKETS = 32
MAX_EXACT = 16
MAX_DISTANCE = 128
PAGE_SIZE = 128
QB = 256
CH = 256
EPS = 1e-6
NEG = -1e30
INT_MIN = -(2 ** 31)
EXP_ZERO_BELOW = -104.0
SCALE = HEAD_DIM ** -0.5
IDX_SCALE = (N_IDX_HEADS * D_IDX) ** -0.5
VMEM_LIMIT = 56 * 1024 * 1024
TOKEN_TILE = 512

KIND_DSA, KIND_MOBA, KIND_SB = 0, 1, 2


def _cparams(*semantics):
    return pltpu.CompilerParams(dimension_semantics=semantics, vmem_limit_bytes=VMEM_LIMIT)


def _dot(a, b):
    return jnp.dot(a, b, preferred_element_type=F32)


def _dot_t(a, b):
    return lax.dot_general(a, b, (((1,), (1,)), ((), ())), preferred_element_type=F32)


def _split_hi_lo(v):
    hi = v.astype(BF16)
    lo = (v - hi.astype(F32)).astype(BF16)
    return hi, lo


def _rms(x, g):
    ms = jnp.mean(x * x, axis=-1, keepdims=True)
    return x * lax.rsqrt(ms + EPS) * g


def _softplus(z):
    return jnp.maximum(z, 0.0) + jnp.log(1.0 + jnp.exp(-jnp.abs(z)))


def _ffn_body(x_ref, g_ref, wgu_ref, wd_ref, o_ref):
    x = x_ref[...]
    xn = _rms(x, g_ref[...]).astype(BF16)
    acc = jnp.zeros_like(x)
    for c in range(D_FF // FF_CHUNK):
        lo = c * FF_CHUNK
        a = _dot(xn, wgu_ref[:, lo:lo + FF_CHUNK])
        b = _dot(xn, wgu_ref[:, D_FF + lo:D_FF + lo + FF_CHUNK])
        h = (a * (1.0 / (1.0 + jnp.exp(-a))) * b).astype(BF16)
        acc = acc + _dot(h, wd_ref[lo:lo + FF_CHUNK, :])
    o_ref[...] = x + 0.5 * acc


def _ffn(x, g, wgu, wd):
    n = x.shape[0]
    tm = min(TOKEN_TILE, n)
    return pl.pallas_call(
        _ffn_body,
        grid=(n // tm,),
        in_specs=[pl.BlockSpec((tm, D_MODEL), lambda i: (i, 0)),
                  pl.BlockSpec((1, D_MODEL), lambda i: (0, 0)),
                  pl.BlockSpec((D_MODEL, 2 * D_FF), lambda i: (0, 0)),
                  pl.BlockSpec((D_FF, D_MODEL), lambda i: (0, 0))],
        out_specs=pl.BlockSpec((tm, D_MODEL), lambda i: (i, 0)),
        out_shape=jax.ShapeDtypeStruct(x.shape, F32),
        compiler_params=_cparams("parallel"),
        name="ffn",
    )(x, g.reshape(1, D_MODEL), wgu, wd)


def _head_rms(y, g_row):
    r_i = lax.broadcasted_iota(I32, (HHD, LANES), 0) >> 6
    c_i = lax.broadcasted_iota(I32, (HHD, LANES), 1)
    gather = jnp.where(r_i == c_i, 1.0, 0.0).astype(BF16)
    r2 = lax.broadcasted_iota(I32, (LANES, HHD), 0)
    c2 = lax.broadcasted_iota(I32, (LANES, HHD), 1) >> 6
    spread = jnp.where(r2 == c2, 1.0, 0.0).astype(BF16)
    hi, lo = _split_hi_lo(y * y)
    ssq = _dot(hi, gather) + _dot(lo, gather)
    r = lax.rsqrt(ssq * (1.0 / HEAD_DIM) + EPS)
    rhi, rlo = _split_hi_lo(r)
    rfull = _dot(rhi, spread) + _dot(rlo, spread)
    return y * rfull * g_row


def _store_pairs(ref, y):
    for p in range(N_PAIRS):
        ref[p] = y[:, p * LANES:(p + 1) * LANES].astype(BF16)


def _proj_body(kind, with_kmean, *refs):
    if kind == KIND_DSA:
        (x_ref, g_ref, w_ref, qg_ref, kg_ref, wt_ref, kig_ref,
         k_out, v_out, q_pm, k_pm, v_pm, vt_pm, qi_pm, ki_out, ki2_out, wi_out) = refs
    elif kind == KIND_MOBA:
        x_ref, g_ref, w_ref, qg_ref, kg_ref, k_out, v_out, q_pm, k_pm, v_pm, vt_pm = refs[:11]
        kmean_out = refs[11] if with_kmean else None
    else:
        x_ref, g_ref, w_ref, k_out, v_out, q_pm, k_pm, v_pm, vt_pm = refs
    xn = _rms(x_ref[...], g_ref[...]).astype(BF16)
    q = _dot(xn, w_ref[:, 0:HHD])
    k = _dot(xn, w_ref[:, HHD:2 * HHD])
    v = _dot(xn, w_ref[:, 2 * HHD:3 * HHD])
    if kind != KIND_SB:
        q = _head_rms(q, qg_ref[...])
        k = _head_rms(k, kg_ref[...])
    k_out[...] = k
    v_out[...] = v
    _store_pairs(q_pm, q * SCALE)
    _store_pairs(k_pm, k)
    _store_pairs(v_pm, v)
    for p in range(N_PAIRS):
        vt_pm[p] = v[:, p * LANES:(p + 1) * LANES].T.astype(BF16)
    if kind == KIND_DSA:
        _store_pairs(qi_pm, _dot(xn, w_ref[:, 3 * HHD:4 * HHD]))
        ki2 = _dot(xn, wt_ref[:, 0:LANES])
        ki2 = ki2 * lax.rsqrt(jnp.mean(ki2 * ki2, axis=-1, keepdims=True) + EPS) * kig_ref[...]
        ki_out[...] = ki2[:, 0:D_IDX]
        ki2_out[...] = ki2.astype(BF16)
        wi_out[...] = _dot(xn, wt_ref[:, LANES:2 * LANES])[:, 0:N_IDX_HEADS] * IDX_SCALE
    if kind == KIND_MOBA and with_kmean:
        tm = k.shape[0]
        for j in range(tm // MOBA_BLOCK):
            kmean_out[j] = jnp.mean(k[j * MOBA_BLOCK:(j + 1) * MOBA_BLOCK], axis=0, keepdims=True)


def _proj(kind, x, g, w, qg=None, kg=None, w_tail=None, kig=None, with_kmean=False):
    n = x.shape[0]
    tm = min(TOKEN_TILE, n)
    row = lambda i: (i, 0)
    fixed = lambda i: (0, 0)
    pm_spec = pl.BlockSpec((N_PAIRS, tm, LANES), lambda i: (0, i, 0))
    pm_shape = jax.ShapeDtypeStruct((N_PAIRS, n, LANES), BF16)
    vt_spec = pl.BlockSpec((N_PAIRS, LANES, tm), lambda i: (0, 0, i))
    vt_shape = jax.ShapeDtypeStruct((N_PAIRS, LANES, n), BF16)
    args = [x, g.reshape(1, D_MODEL), w]
    in_specs = [pl.BlockSpec((tm, D_MODEL), row), pl.BlockSpec((1, D_MODEL), fixed),
                pl.BlockSpec(w.shape, fixed)]
    if kind != KIND_SB:
        args += [jnp.tile(qg, N_HEADS).reshape(1, HHD), jnp.tile(kg, N_HEADS).reshape(1, HHD)]
        in_specs += [pl.BlockSpec((1, HHD), fixed)] * 2
    if kind == KIND_DSA:
        args += [w_tail, jnp.tile(kig, LANES // D_IDX).reshape(1, LANES)]
        in_specs += [pl.BlockSpec(w_tail.shape, fixed), pl.BlockSpec((1, LANES), fixed)]
    out_shape = [jax.ShapeDtypeStruct((n, HHD), F32)] * 2 + [pm_shape] * 3 + [vt_shape]
    out_specs = [pl.BlockSpec((tm, HHD), row)] * 2 + [pm_spec] * 3 + [vt_spec]
    if kind == KIND_DSA:
        out_shape += [pm_shape, jax.ShapeDtypeStruct((n, D_IDX), F32),
                      jax.ShapeDtypeStruct((n, LANES), BF16),
                      jax.ShapeDtypeStruct((n, N_IDX_HEADS), F32)]
        out_specs += [pm_spec, pl.BlockSpec((tm, D_IDX), row), pl.BlockSpec((tm, LANES), row),
                      pl.BlockSpec((tm, N_IDX_HEADS), row)]
    if kind == KIND_MOBA and with_kmean:
        out_shape += [jax.ShapeDtypeStruct((n // MOBA_BLOCK, 1, HHD), F32)]
        out_specs += [pl.BlockSpec((tm // MOBA_BLOCK, 1, HHD), lambda i: (i, 0, 0))]
    return pl.pallas_call(
        functools.partial(_proj_body, kind, with_kmean),
        grid=(n // tm,), in_specs=in_specs, out_specs=out_specs, out_shape=out_shape,
        compiler_params=_cparams("parallel"), name="proj",
    )(*args)


def _outproj_body(x_ref, o_ref, w_ref, y_ref):
    o = jnp.concatenate([o_ref[p] for p in range(N_PAIRS)], axis=1)
    y_ref[...] = x_ref[...] + _dot(o, w_ref[...])


def _outproj(x, o_pm, w):
    n = x.shape[0]
    tm = min(TOKEN_TILE, n)
    return pl.pallas_call(
        _outproj_body,
        grid=(n // tm,),
        in_specs=[pl.BlockSpec((tm, D_MODEL), lambda i: (i, 0)),
                  pl.BlockSpec((N_PAIRS, tm, LANES), lambda i: (0, i, 0)),
                  pl.BlockSpec((HHD, D_MODEL), lambda i: (0, 0))],
        out_specs=pl.BlockSpec((tm, D_MODEL), lambda i: (i, 0)),
        out_shape=jax.ShapeDtypeStruct(x.shape, F32),
        compiler_params=_cparams("parallel"), name="outproj",
    )(x, o_pm, w)


def _sortable_key(score):
    bits = pltpu.bitcast(score + 0.0, I32)
    return bits ^ ((bits >> 31) & 0x7FFFFFFF)


def _topk_threshold(count_ge, shape, topk):
    zero = jnp.zeros(shape, I32)
    t0 = jnp.where(count_ge(zero) >= topk, zero, jnp.full(shape, INT_MIN, I32))

    def bit_step(i, t):
        cand = t + jnp.left_shift(jnp.int32(1), jnp.int32(30) - i)
        return jnp.where(count_ge(cand) >= topk, cand, t)

    return lax.fori_loop(0, 31, bit_step, t0)


def _top3_flags(gate, axis):
    idx = lax.broadcasted_iota(I32, gate.shape, axis)
    sel = jnp.zeros(gate.shape, F32)
    for _ in range(MOBA_TOPK):
        mx = jnp.max(gate, axis=axis, keepdims=True)
        cand = jnp.where((gate == mx) & (gate > -jnp.inf), idx, gate.shape[axis])
        first = jnp.min(cand, axis=axis, keepdims=True)
        pick = idx == first
        sel = jnp.where(pick, 1.0, sel)
        gate = jnp.where(pick, -jnp.inf, gate)
    return sel


def _stack_masked_q(q2):
    lane = lax.broadcasted_iota(I32, q2.shape, 1)
    zero = jnp.zeros_like(q2)
    return jnp.concatenate([jnp.where(lane < HEAD_DIM, q2, zero),
                            jnp.where(lane >= HEAD_DIM, q2, zero)], axis=0)


def _pair_out(acc_p, scale_p=None):
    a, b = acc_p[0:HEAD_DIM, 0:QB], acc_p[HEAD_DIM:LANES, QB:2 * QB]
    if scale_p is not None:
        a, b = a * scale_p[:, 0:QB], b * scale_p[:, QB:2 * QB]
    return jnp.concatenate([a, b], axis=0).T.astype(BF16)


def _flash_step_t(s_t, m, l, acc, v_t):
    m_new = jnp.maximum(m, jnp.max(s_t, axis=0, keepdims=True))
    alpha = jnp.exp(m - m_new)
    p = jnp.exp(s_t - m_new)
    l = alpha * l + jnp.sum(p, axis=0, keepdims=True)
    acc = alpha * acc + _dot(v_t, p.astype(BF16))
    return m_new, l, acc


def _dsa_mask_body(topk, nq, qi_ref, ki2_ref, wit_ref, o_ref, key_ref, qim_ref):
    qb = pl.program_id(1)
    nck = qb + 1
    for p in range(N_PAIRS):
        qim_ref[p] = _stack_masked_q(qi_ref[p])
    wt = wit_ref[...]
    rowk = lax.broadcasted_iota(I32, (CH, QB), 0)
    laneq = lax.broadcasted_iota(I32, (CH, QB), 1)

    def score_chunk(c, carry):
        ki = ki2_ref[pl.ds(pl.multiple_of(c * CH, CH), CH), :]
        s = jnp.zeros((CH, QB), F32)
        for p in range(N_PAIRS):
            d = jnp.maximum(_dot_t(ki, qim_ref[p]), 0.0)
            s = s + wt[2 * p:2 * p + 1] * d[:, 0:QB] + wt[2 * p + 1:2 * p + 2] * d[:, QB:2 * QB]
        s = jnp.where(c * CH + rowk <= qb * QB + laneq, s, -jnp.inf)
        key_ref[c] = _sortable_key(s)
        return carry

    lax.fori_loop(0, nck, score_chunk, 0)

    def count_ge(cand):
        def body(c, acc):
            hit = jnp.where(key_ref[c] >= cand, 1.0, 0.0)
            return acc + jnp.sum(hit.reshape(CH // 8, 8, QB), axis=0)

        acc = lax.fori_loop(0, nck, body, jnp.zeros((8, QB), F32))
        return jnp.sum(acc, axis=0, keepdims=True)

    thr = _topk_threshold(count_ge, (1, QB), float(topk))

    def mask_chunk(c, carry):
        o_ref[pl.ds(pl.multiple_of(c * CH, CH), CH), :] = jnp.where(key_ref[c] >= thr, 0.0, NEG).astype(BF16)
        return carry

    lax.fori_loop(0, nck, mask_chunk, 0)

    def fill_chunk(c, carry):
        o_ref[pl.ds(pl.multiple_of(c * CH, CH), CH), :] = jnp.full((CH, QB), NEG, BF16)
        return carry

    lax.fori_loop(nck, nq, fill_chunk, 0)


def _dsa_mask(batch, seq, qi_pm, ki2, wi_t):
    nq = seq // QB
    topk = min(DSA_TOPK, seq // 4)
    return pl.pallas_call(
        functools.partial(_dsa_mask_body, topk, nq),
        grid=(batch, nq),
        in_specs=[pl.BlockSpec((N_PAIRS, QB, LANES), lambda b, i: (0, b * nq + i, 0)),
                  pl.BlockSpec((seq, LANES), lambda b, i: (b, 0)),
                  pl.BlockSpec((N_IDX_HEADS, QB), lambda b, i: (0, b * nq + i))],
        out_specs=pl.BlockSpec((None, seq, QB), lambda b, i: (b, 0, i)),
        out_shape=jax.ShapeDtypeStruct((batch, seq, seq), BF16),
        scratch_shapes=[pltpu.VMEM((nq, CH, QB), I32), pltpu.VMEM((N_PAIRS, 2 * QB, LANES), BF16)],
        compiler_params=_cparams("parallel", "parallel"), name="dsa_mask",
    )(qi_pm, ki2, wi_t)


def _masked_flash_body(kind, q_ref, k_ref, vt_ref, bias_ref, x_ref, o_ref,
                       qmm_ref, m_ref, l_ref, acc_ref, *sel):
    qb = pl.program_id(1)
    c = pl.program_id(2)

    @pl.when(c == 0)
    def _init():
        for p in range(N_PAIRS):
            qmm = _stack_masked_q(q_ref[p])
            qmm_ref[p] = qmm
            m_ref[p] = jnp.full((1, 2 * QB), NEG, F32)
            l_ref[p] = jnp.zeros((1, 2 * QB), F32)
            acc_ref[p] = jnp.zeros((LANES, 2 * QB), F32)
            if kind == KIND_MOBA:
                sel_ref = sel[0]
                n_rows = sel_ref.shape[1]
                gate = _dot_t(x_ref[p], qmm)[0:n_rows]
                blk = lax.broadcasted_iota(I32, gate.shape, 0)
                sel_ref[p] = _top3_flags(jnp.where(blk < qb, gate, -jnp.inf), 0)

    @pl.when(c <= qb)
    def _step():
        if kind == KIND_DSA:
            mask = x_ref[...].astype(F32)
            mask2 = jnp.concatenate([mask, mask], axis=1)
        for p in range(N_PAIRS):
            s_t = _dot_t(k_ref[p], qmm_ref[p]) + bias_ref[p]
            if kind == KIND_DSA:
                s_t = s_t + mask2
            else:
                vis = sel[0][p, pl.ds(c, 1), :] + jnp.where(c == qb, 1.0, 0.0)
                s_t = s_t + jnp.where(vis > 0.5, 0.0, NEG)
            m, l, acc = _flash_step_t(s_t, m_ref[p], l_ref[p], acc_ref[p], vt_ref[p])
            m_ref[p] = m
            l_ref[p] = l
            acc_ref[p] = acc

    @pl.when(c == qb)
    def _finish():
        for p in range(N_PAIRS):
            o_ref[p] = _pair_out(acc_ref[p], 1.0 / l_ref[p])


def _masked_flash(kind, batch, seq, q_pm, k_pm, vt_pm, bias, extra):
    nq = seq // QB
    q_spec = pl.BlockSpec((N_PAIRS, QB, LANES), lambda b, i, c: (0, b * nq + i, 0))
    in_specs = [q_spec,
                pl.BlockSpec((N_PAIRS, CH, LANES), lambda b, i, c: (0, b * nq + jnp.minimum(c, i), 0)),
                pl.BlockSpec((N_PAIRS, LANES, CH), lambda b, i, c: (0, 0, b * nq + jnp.minimum(c, i))),
                pl.BlockSpec((N_PAIRS, None, CH, 2 * QB),
                             lambda b, i, c: (0, jnp.clip(i - c, 0, 2), 0, 0))]
    scratch = [pltpu.VMEM((N_PAIRS, 2 * QB, LANES), BF16), pltpu.VMEM((N_PAIRS, 1, 2 * QB), F32),
               pltpu.VMEM((N_PAIRS, 1, 2 * QB), F32), pltpu.VMEM((N_PAIRS, LANES, 2 * QB), F32)]
    if kind == KIND_DSA:
        in_specs.append(pl.BlockSpec((None, CH, QB), lambda b, i, c: (b, jnp.minimum(c, i), i)))
    else:
        in_specs.append(pl.BlockSpec((N_PAIRS, None, LANES, LANES), lambda b, i, c: (0, b, 0, 0)))
        scratch.append(pltpu.VMEM((N_PAIRS, -(-nq // 8) * 8, 2 * QB), F32))
    return pl.pallas_call(
        functools.partial(_masked_flash_body, kind),
        grid=(batch, nq, nq), in_specs=in_specs, out_specs=q_spec,
        out_shape=jax.ShapeDtypeStruct((N_PAIRS, batch * seq, LANES), BF16),
        scratch_shapes=scratch,
        compiler_params=_cparams("parallel", "parallel", "arbitrary"), name="masked_flash",
    )(q_pm, k_pm, vt_pm, bias, extra)


def _sb_flash_body(q_ref, k_ref, vt_ref, o_ref, qmm_ref, tail_ref, acc_ref, live_ref):
    qb = pl.program_id(1)
    c = pl.program_id(2)

    @pl.when(c == 0)
    def _init():
        for p in range(N_PAIRS):
            qmm_ref[p] = _stack_masked_q(q_ref[p])
            tail_ref[p] = jnp.zeros((1, 2 * QB), F32)
            acc_ref[p] = jnp.zeros((LANES, 2 * QB), F32)
        live_ref[0] = 1

    @pl.when((c <= qb) & (live_ref[0] == 1))
    def _step():
        rowk = lax.broadcasted_iota(I32, (CH, 2 * QB), 0)
        laneq = lax.broadcasted_iota(I32, (CH, 2 * QB), 1) & (QB - 1)
        strict = (qb - c) * CH + rowk < qb * QB + laneq
        later = (lax.broadcasted_iota(I32, (CH, CH), 1) > lax.broadcasted_iota(I32, (CH, CH), 0))
        later = jnp.where(later, 1.0, 0.0).astype(BF16)
        worst = jnp.full((1, 1), -jnp.inf, F32)
        for p in range(N_PAIRS):
            z = _dot_t(k_ref[p], qmm_ref[p])
            sp = _softplus(z)
            lr = jnp.where(strict, -sp, 0.0)
            hi, lo = _split_hi_lo(lr)
            tail = tail_ref[p]
            a = jnp.where(strict, jnp.exp(z - sp + tail + _dot(later, hi) + _dot(later, lo)), 0.0)
            acc_ref[p] = acc_ref[p] + _dot(vt_ref[p], a.astype(BF16))
            tail = tail + jnp.sum(lr, axis=0, keepdims=True)
            tail_ref[p] = tail
            worst = jnp.maximum(worst, jnp.max(tail, axis=1, keepdims=True))
        live_ref[0] = jnp.where(worst[0, 0] > EXP_ZERO_BELOW, 1, 0)

    @pl.when(c == qb)
    def _finish():
        for p in range(N_PAIRS):
            o_ref[p] = _pair_out(acc_ref[p])


def _sb_flash(batch, seq, q_pm, k_pm, vt_pm):
    nq = seq // QB
    q_spec = pl.BlockSpec((N_PAIRS, QB, LANES), lambda b, i, c: (0, b * nq + i, 0))
    return pl.pallas_call(
        _sb_flash_body,
        grid=(batch, nq, nq),
        in_specs=[q_spec,
                  pl.BlockSpec((N_PAIRS, CH, LANES), lambda b, i, c: (0, b * nq + jnp.maximum(i - c, 0), 0)),
                  pl.BlockSpec((N_PAIRS, LANES, CH), lambda b, i, c: (0, 0, b * nq + jnp.maximum(i - c, 0)))],
        out_specs=q_spec,
        out_shape=jax.ShapeDtypeStruct((N_PAIRS, batch * seq, LANES), BF16),
        scratch_shapes=[pltpu.VMEM((N_PAIRS, 2 * QB, LANES), BF16), pltpu.VMEM((N_PAIRS, 1, 2 * QB), F32),
                        pltpu.VMEM((N_PAIRS, LANES, 2 * QB), F32), pltpu.SMEM((1,), I32)],
        compiler_params=_cparams("parallel", "parallel", "arbitrary"), name="sb_flash",
    )(q_pm, k_pm, vt_pm)


N_ROWS = 64
PAGES_PER_STEP = 2
WIDE = PAGE_SIZE * N_HEADS
NEW_KEYS = 8
NEW_WIDE = NEW_KEYS * N_HEADS


def _softmax_update(s, v_bf16, m_ref, l_ref, acc_ref):
    m_old = m_ref[...]
    m_new = jnp.maximum(m_old, jnp.max(s, axis=1, keepdims=True))
    alpha = jnp.exp(m_old - m_new)
    p = jnp.exp(s - m_new)
    m_ref[...] = m_new
    l_ref[...] = alpha * l_ref[...] + jnp.sum(p, axis=1, keepdims=True)
    acc_ref[...] = alpha * acc_ref[...] + _dot(p.astype(BF16), v_bf16)


def _own_head_lanes(shape):
    row = lax.broadcasted_iota(I32, shape, 0)
    lane = lax.broadcasted_iota(I32, shape, 1)
    return (lane & (N_HEADS - 1)) == (row & (N_HEADS - 1))


def _dec_init(m_ref, l_ref, acc_ref):
    m_ref[...] = jnp.full((N_ROWS, 1), NEG, F32)
    l_ref[...] = jnp.zeros((N_ROWS, 1), F32)
    acc_ref[...] = jnp.zeros(acc_ref.shape, F32)


def _rows_from_tokens(x8, dec_seq):
    return jnp.concatenate([jnp.broadcast_to(x8[t:t + 1], (N_HEADS, x8.shape[1]))
                            for t in range(dec_seq)], axis=0)


def _dsa_sample_body(n_pages, dec_seq, topk, pt_ref, q_ref, qi_ref, wib_ref, k0, k1, v0, v1, *refs):
    ki_pages = refs[0:n_pages]
    (knew_ref, vnew_ref, kinew_ref, bias_ref, biasn_ref, spread_ref, o_ref,
     flag_ref, m_ref, l_ref, acc_ref) = refs[n_pages:]
    j = pl.program_id(1)
    last = n_pages // PAGES_PER_STEP - 1
    q = q_ref[...]

    def selected(pg, width):
        flags = _rows_from_tokens(flag_ref[pg], dec_seq).astype(BF16)
        return jnp.where(_dot(flags, spread_ref[:, 0:width]) > 0.5, 0.0, NEG)

    @pl.when(j == 0)
    def _select():
        qi = qi_ref[...]
        wib = wib_ref[...]
        rows8 = lax.broadcasted_iota(I32, (8, PAGE_SIZE), 0)
        lane8 = lax.broadcasted_iota(I32, (8, PAGE_SIZE), 1)
        keys = []
        for pg in range(n_pages + 1):
            ki = ki_pages[pg][...].astype(BF16) if pg < n_pages else kinew_ref[...]
            wr = wib * jnp.maximum(_dot_t(qi, ki), 0.0)
            per_t = [jnp.sum(wr[t * N_HEADS:(t + 1) * N_HEADS], axis=0, keepdims=True)
                     for t in range(dec_seq)]
            sc = jnp.concatenate(per_t + [jnp.zeros((8 - dec_seq, PAGE_SIZE), F32)], axis=0)
            valid = rows8 < dec_seq
            if pg == n_pages:
                valid = valid & (lane8 <= rows8)
            keys.append(_sortable_key(jnp.where(valid, sc, -jnp.inf)))

        def count_ge(cand):
            acc = jnp.zeros((8, PAGE_SIZE), F32)
            for kk in keys:
                acc = acc + jnp.where(kk >= cand, 1.0, 0.0)
            return jnp.sum(acc, axis=1, keepdims=True)

        thr = _topk_threshold(count_ge, (8, 1), float(topk))
        for pg in range(n_pages + 1):
            flag_ref[pg] = jnp.where(keys[pg] >= thr, 1.0, 0.0)
        _dec_init(m_ref, l_ref, acc_ref)

    for r, (k_ref, v_ref) in enumerate(((k0, v0), (k1, v1))):
        tile = jnp.where(j == last, 1, 0) if r == PAGES_PER_STEP - 1 else 0
        s = (_dot_t(q, k_ref[...].astype(BF16)) + bias_ref[tile]
             + selected(PAGES_PER_STEP * j + r, WIDE))
        _softmax_update(s, v_ref[...].astype(BF16), m_ref, l_ref, acc_ref)

    @pl.when(j == last)
    def _finish():
        s = _dot_t(q, knew_ref[...]) + biasn_ref[...] + selected(n_pages, NEW_WIDE)
        _softmax_update(s, vnew_ref[...], m_ref, l_ref, acc_ref)
        o_ref[...] = acc_ref[...] / l_ref[...]


def _moba_sample_body(n_pages, pt_ref, q_ref, k0, k1, v0, v1, knew_ref, vnew_ref, bias_ref, biasn_ref,
                      o_ref, gate_ref, pm_ref, pl_ref, pacc_ref, m_ref, l_ref, acc_ref):
    j = pl.program_id(1)
    n_blk = n_pages // PAGES_PER_STEP
    last = n_blk - 1
    q = q_ref[...]
    lane = lax.broadcasted_iota(I32, (N_ROWS, LANES), 1)

    @pl.when(j == 0)
    def _zero():
        gate_ref[...] = jnp.zeros((N_ROWS, LANES), F32)

    _dec_init(m_ref, l_ref, acc_ref)
    k0f, k1f = k0[...], k1[...]
    ksum = (jnp.sum(k0f.reshape(PAGE_SIZE, N_HEADS, HEAD_DIM), axis=0)
            + jnp.sum(k1f.reshape(PAGE_SIZE, N_HEADS, HEAD_DIM), axis=0))
    kmean = jnp.concatenate([ksum * (1.0 / MOBA_BLOCK),
                             jnp.zeros((LANES - N_HEADS, HEAD_DIM), F32)], axis=0).astype(BF16)
    g = jnp.where(_own_head_lanes((N_ROWS, LANES)), _dot_t(q, kmean), 0.0)
    gate_ref[...] = gate_ref[...] + jnp.where(lane == j, jnp.sum(g, axis=1, keepdims=True), 0.0)
    _softmax_update(_dot_t(q, k0f.astype(BF16)) + bias_ref[0], v0[...].astype(BF16), m_ref, l_ref, acc_ref)
    _softmax_update(_dot_t(q, k1f.astype(BF16)) + bias_ref[jnp.where(j == last, 1, 0)],
                    v1[...].astype(BF16), m_ref, l_ref, acc_ref)
    pm_ref[j] = m_ref[...]
    pl_ref[j] = l_ref[...]
    pacc_ref[j] = acc_ref[...]

    @pl.when(j == last)
    def _finish():
        sel = _top3_flags(jnp.where(lane < n_blk, gate_ref[...], -jnp.inf), 1)
        _dec_init(m_ref, l_ref, acc_ref)
        _softmax_update(_dot_t(q, knew_ref[...]) + biasn_ref[...], vnew_ref[...], m_ref, l_ref, acc_ref)
        picked = [sel[:, n:n + 1] > 0.5 for n in range(n_blk)]
        m_all = m_ref[...]
        for n in range(n_blk):
            m_all = jnp.maximum(m_all, jnp.where(picked[n], pm_ref[n], NEG))
        w_own = jnp.exp(m_ref[...] - m_all)
        l_all = w_own * l_ref[...]
        acc = w_own * acc_ref[...]
        for n in range(n_blk):
            w = jnp.where(picked[n], jnp.exp(pm_ref[n] - m_all), 0.0)
            l_all = l_all + w * pl_ref[n]
            acc = acc + w * pacc_ref[n]
        o_ref[...] = acc / l_all


def _sb_sample_body(n_pages, pt_ref, q_ref, k0, k1, v0, v1, knew_ref, vnew_ref, o_ref,
                    tail_ref, acc_ref, live_ref):
    j = pl.program_id(1)
    q = q_ref[...]
    sub = 2 * LANES

    def chunk(z, keep, v_bf16):
        n = z.shape[1]
        step = min(sub, n)
        jj = lax.broadcasted_iota(I32, (step, step + LANES), 0)
        ss = lax.broadcasted_iota(I32, (step, step + LANES), 1)
        ue = jnp.where((jj > ss) | (ss >= step), 1.0, 0.0).astype(BF16)
        sp = _softplus(z)
        lr = jnp.where(keep, -sp, 0.0)
        log_sig = z - sp
        tail = tail_ref[...]
        pieces = [None] * (n // step)
        for i in reversed(range(n // step)):
            sl = slice(i * step, (i + 1) * step)
            hi, lo = _split_hi_lo(lr[:, sl])
            cs = _dot(hi, ue) + _dot(lo, ue)
            tail_b = jnp.concatenate([tail] * (step // LANES), axis=1)
            pieces[i] = jnp.where(keep[:, sl], jnp.exp(log_sig[:, sl] + tail_b + cs[:, :step]), 0.0)
            tail = tail + cs[:, step:]
        a = jnp.concatenate(pieces, axis=1)
        acc_ref[...] = acc_ref[...] + _dot(a.astype(BF16), v_bf16)
        tail_ref[...] = tail

    @pl.when(j == 0)
    def _newest():
        tail_ref[...] = jnp.zeros((N_ROWS, LANES), F32)
        acc_ref[...] = jnp.zeros((N_ROWS, HEAD_DIM), F32)
        live_ref[0] = 1
        row = lax.broadcasted_iota(I32, (N_ROWS, NEW_WIDE), 0)
        lane = lax.broadcasted_iota(I32, (N_ROWS, NEW_WIDE), 1)
        before = (lane >> 4) < (row >> 4)
        chunk(_dot_t(q, knew_ref[...]), _own_head_lanes((N_ROWS, NEW_WIDE)) & before, vnew_ref[...])

    @pl.when(live_ref[0] == 1)
    def _pages():
        own = _own_head_lanes((N_ROWS, WIDE))
        chunk(_dot_t(q, k1[...].astype(BF16)), own, v1[...].astype(BF16))
        chunk(_dot_t(q, k0[...].astype(BF16)), own, v0[...].astype(BF16))
        worst = jnp.max(jnp.max(tail_ref[...], axis=1, keepdims=True), axis=0, keepdims=True)
        live_ref[0] = jnp.where(worst[0, 0] > EXP_ZERO_BELOW, 1, 0)

    @pl.when(j == n_pages // PAGES_PER_STEP - 1)
    def _finish():
        o_ref[...] = acc_ref[...]


def _sample_attn(kind, layer, page_table, cache_k, cache_v, q64, knew, vnew, bias=None, dsa=None):
    n_seq, n_pages = page_table.shape
    n_steps = n_pages // PAGES_PER_STEP
    dec_seq = N_ROWS // N_HEADS

    def seq_spec(shape):
        return pl.BlockSpec((None,) + tuple(shape), lambda b, j, pt: (b,) + (0,) * len(shape))

    def const_spec(shape):
        return pl.BlockSpec(tuple(shape), lambda b, j, pt: (0,) * len(shape))

    def page_spec(lyr, width, n_rows, page_of):
        return pl.BlockSpec((None, None, n_rows, width),
                            lambda b, j, pt: (lyr, pt[b, page_of(j)], 0, 0))

    if kind == KIND_SB:
        pages = [lambda j: n_pages - 2 - 2 * j, lambda j: n_pages - 1 - 2 * j]
    else:
        pages = [lambda j: 2 * j, lambda j: 2 * j + 1]
    args = [q64]
    in_specs = [seq_spec((N_ROWS, HEAD_DIM))]
    if kind == KIND_DSA:
        qi_r, wib, cache_ki, slot, kinew = dsa
        args += [qi_r, wib]
        in_specs += [seq_spec((N_ROWS, D_IDX)), seq_spec((N_ROWS, PAGE_SIZE))]
    args += [cache_k, cache_k, cache_v, cache_v]
    in_specs += [page_spec(layer, HEAD_DIM, WIDE, pages[0]), page_spec(layer, HEAD_DIM, WIDE, pages[1])] * 2
    if kind == KIND_DSA:
        args += [cache_ki] * n_pages
        in_specs += [page_spec(slot, D_IDX, PAGE_SIZE, functools.partial(lambda j, pg: pg, pg=pg))
                     for pg in range(n_pages)]
    args += [knew, vnew]
    in_specs += [seq_spec((NEW_WIDE, HEAD_DIM))] * 2
    if kind == KIND_DSA:
        args += [kinew]
        in_specs += [seq_spec((PAGE_SIZE, D_IDX))]
    if kind != KIND_SB:
        bias_pages, bias_new = bias
        args += [bias_pages, bias_new]
        in_specs += [const_spec((2, N_ROWS, WIDE)), const_spec((N_ROWS, NEW_WIDE))]
    if kind == KIND_DSA:
        keys = jnp.arange(PAGE_SIZE, dtype=I32)[:, None]
        lanes = jnp.arange(WIDE, dtype=I32)[None, :] // N_HEADS
        args += [jnp.where(keys == lanes, 1.0, 0.0).astype(BF16)]
        in_specs += [const_spec((PAGE_SIZE, WIDE))]
    acc_shape = pltpu.VMEM((N_ROWS, HEAD_DIM), F32)
    stat_shape = pltpu.VMEM((N_ROWS, 1), F32)
    if kind == KIND_DSA:
        topk = min(DSA_TOPK, (n_pages * PAGE_SIZE + dec_seq) // 4)
        body = functools.partial(_dsa_sample_body, n_pages, dec_seq, topk)
        scratch = [pltpu.VMEM((n_pages + 1, 8, PAGE_SIZE), F32), stat_shape, stat_shape, acc_shape]
    elif kind == KIND_MOBA:
        body = functools.partial(_moba_sample_body, n_pages)
        scratch = [pltpu.VMEM((N_ROWS, LANES), F32),
                   pltpu.VMEM((n_steps, N_ROWS, 1), F32), pltpu.VMEM((n_steps, N_ROWS, 1), F32),
                   pltpu.VMEM((n_steps, N_ROWS, HEAD_DIM), F32),
                   stat_shape, stat_shape, acc_shape]
    else:
        body = functools.partial(_sb_sample_body, n_pages)
        scratch = [pltpu.VMEM((N_ROWS, LANES), F32), acc_shape, pltpu.SMEM((1,), I32)]
    return pl.pallas_call(
        body,
        grid_spec=pltpu.PrefetchScalarGridSpec(
            num_scalar_prefetch=1, grid=(n_seq, n_steps), in_specs=in_specs,
            out_specs=seq_spec((N_ROWS, HEAD_DIM)), scratch_shapes=scratch),
        out_shape=jax.ShapeDtypeStruct((n_seq, N_ROWS, HEAD_DIM), F32),
        compiler_params=_cparams("parallel", "arbitrary"), name="sample_attn",
    )(page_table, *args)


def _rel_bucket(dist):
    n = jnp.maximum(dist, 0)
    nf = jnp.maximum(n, 1).astype(F32)
    large = MAX_EXACT + (jnp.log(nf / MAX_EXACT) / math.log(MAX_DISTANCE / MAX_EXACT)
                         * (N_BUCKETS - MAX_EXACT)).astype(I32)
    large = jnp.minimum(large, N_BUCKETS - 1)
    return jnp.where(n < MAX_EXACT, n, large)


def _bias_lookup(rel_table, dist, shift=None):
    b = jnp.moveaxis(rel_table[_rel_bucket(dist)], -1, 0)
    if shift is not None:
        b = b - shift.reshape((N_HEADS,) + (1,) * dist.ndim)
    return jnp.where(dist[None] >= 0, b, NEG)


def _prompt_bias_tiles(rel_table):
    far = rel_table[_rel_bucket(jnp.int32(2 * QB))]
    j = jnp.arange(CH, dtype=I32)[:, None]
    i = jnp.arange(QB, dtype=I32)[None, :]
    tiles = jnp.stack([_bias_lookup(rel_table, u * CH + i - j, far) for u in range(3)], axis=1)
    tiles = tiles.reshape(N_PAIRS, 2, 3, CH, QB)
    return jnp.concatenate([tiles[:, 0], tiles[:, 1]], axis=-1).astype(F32)


def _wide_tile(vals):
    _, n_t, n_s = vals.shape
    heads = jnp.arange(N_HEADS)
    own = heads[None, :, None, None] == heads[None, None, None, :]
    wide = jnp.where(own, jnp.transpose(vals, (1, 0, 2))[..., None], NEG)
    return wide.reshape(n_t * N_HEADS, n_s * N_HEADS).astype(F32)


def _sample_bias_tiles(rel_table, dec_seq):
    t = jnp.arange(dec_seq, dtype=I32)[:, None]
    j = jnp.arange(PAGE_SIZE, dtype=I32)[None, :]
    far = _bias_lookup(rel_table, jnp.full((dec_seq, PAGE_SIZE), 2 * PAGE_SIZE, I32))
    last = _bias_lookup(rel_table, PAGE_SIZE + t - j)
    jn = jnp.arange(NEW_KEYS, dtype=I32)[None, :]
    new = _bias_lookup(rel_table, jnp.where(jn < dec_seq, t - jn, -1))
    return jnp.stack([_wide_tile(far), _wide_tile(last)], axis=0), _wide_tile(new)


def _from_pairs(y_pm):
    return jnp.transpose(y_pm, (1, 0, 2)).reshape(y_pm.shape[1], HHD)


def _to_pairs(y):
    return jnp.transpose(y.reshape(y.shape[0], N_PAIRS, LANES), (1, 0, 2))


def _new_chunk_rows(y, n_seq, dec_seq):
    y4 = y.reshape(n_seq, dec_seq, N_HEADS, HEAD_DIM)
    y4 = jnp.pad(y4, ((0, 0), (0, NEW_KEYS - dec_seq), (0, 0), (0, 0)))
    return y4.reshape(n_seq, NEW_WIDE, HEAD_DIM)


def kernel(x_prompt, x_sample, cache_k, cache_v, cache_kidx, page_table, rel_table,
           norm_ffn1, ffn1_w_gu, ffn1_w_down, norm_mix, norm_ffn2, ffn2_w_gu, ffn2_w_down,
           a_w_in, a_w_out, a_q_norm, a_k_norm, a_kidx_norm,
           b_w_in, b_w_out, b_q_norm, b_k_norm, c_w_in, c_w_out):
    batch, seq, _ = x_prompt.shape
    n_seq, dec_seq, _ = x_sample.shape
    depth = cache_k.shape[0]
    n_pool = cache_k.shape[1]
    n_pages = page_table.shape[1]
    assert seq % TOKEN_TILE == 0 and QB == CH == MOBA_BLOCK and seq % QB == 0
    assert dec_seq * N_HEADS == N_ROWS and n_pages % PAGES_PER_STEP == 0
    assert PAGES_PER_STEP * PAGE_SIZE == MOBA_BLOCK and cache_k.shape[2] == PAGE_SIZE

    xp = x_prompt.reshape(batch * seq, D_MODEL)
    xs = x_sample.reshape(n_seq * dec_seq, D_MODEL)
    cache_k2 = cache_k.reshape(depth, n_pool, PAGE_SIZE * N_HEADS, HEAD_DIM)
    cache_v2 = cache_v.reshape(depth, n_pool, PAGE_SIZE * N_HEADS, HEAD_DIM)
    bias_p = _prompt_bias_tiles(rel_table)
    bias_s = _sample_bias_tiles(rel_table, dec_seq)
    n_blk = seq // MOBA_BLOCK

    new_k_p, new_v_p, new_ki_p, new_k_s, new_v_s, new_ki_s = [], [], [], [], [], []
    counts = [0, 0, 0]
    for i in range(depth):
        kind = i % 3
        slot = counts[kind]
        counts[kind] += 1
        wgu1, wd1 = ffn1_w_gu[i].astype(BF16), ffn1_w_down[i].astype(BF16)
        wgu2, wd2 = ffn2_w_gu[i].astype(BF16), ffn2_w_down[i].astype(BF16)
        xp = _ffn(xp, norm_ffn1[i], wgu1, wd1)
        xs = _ffn(xs, norm_ffn1[i], wgu1, wd1)

        dsa = None
        if kind == KIND_DSA:
            w_in = a_w_in[slot]
            w_main = w_in[:, :4 * HHD].astype(BF16)
            w_ki = w_in[:, 4 * HHD:4 * HHD + D_IDX]
            w_wi = jnp.pad(w_in[:, 4 * HHD + D_IDX:], ((0, 0), (0, LANES - N_IDX_HEADS)))
            w_tail = jnp.concatenate([w_ki, w_ki, w_wi], axis=1).astype(BF16)
            kw = dict(qg=a_q_norm[slot], kg=a_k_norm[slot], w_tail=w_tail, kig=a_kidx_norm[slot])
            kp, vp, qp_pm, kp_pm, _, vtp_pm, qip_pm, kip, ki2p, wip = _proj(kind, xp, norm_mix[i], w_main, **kw)
            ks, vs, qs_pm, ks_pm, vs_pm, _, qis_pm, kis, ki2s, wis = _proj(kind, xs, norm_mix[i], w_main, **kw)
            mask = _dsa_mask(batch, seq, qip_pm, ki2p, wip.T)
            op_pm = _masked_flash(kind, batch, seq, qp_pm, kp_pm, vtp_pm, bias_p, mask)
            qi_r = _from_pairs(qis_pm).reshape(n_seq, N_ROWS, D_IDX)
            wib = jnp.broadcast_to(wis.reshape(n_seq, N_ROWS, 1), (n_seq, N_ROWS, PAGE_SIZE))
            kinew = jnp.pad(ki2s[:, :D_IDX].reshape(n_seq, dec_seq, D_IDX),
                            ((0, 0), (0, PAGE_SIZE - dec_seq), (0, 0)))
            dsa = (qi_r, wib, cache_kidx, slot, kinew)
            w_out = a_w_out[slot]
            new_ki_p.append(kip.reshape(batch, seq, D_IDX))
            new_ki_s.append(kis.reshape(n_seq, dec_seq, D_IDX))
        elif kind == KIND_MOBA:
            kw = dict(qg=b_q_norm[slot], kg=b_k_norm[slot])
            w_in = b_w_in[slot].astype(BF16)
            kp, vp, qp_pm, kp_pm, _, vtp_pm, kmean = _proj(kind, xp, norm_mix[i], w_in, with_kmean=True, **kw)
            ks, vs, qs_pm, ks_pm, vs_pm, _ = _proj(kind, xs, norm_mix[i], w_in, **kw)
            km = jnp.transpose(kmean.reshape(batch, n_blk, N_PAIRS, LANES), (2, 0, 1, 3))
            km = jnp.pad(km, ((0, 0), (0, 0), (0, LANES - n_blk), (0, 0))).astype(BF16)
            op_pm = _masked_flash(kind, batch, seq, qp_pm, kp_pm, vtp_pm, bias_p, km)
            w_out = b_w_out[slot]
        else:
            w_in = c_w_in[slot].astype(BF16)
            kp, vp, qp_pm, kp_pm, _, vtp_pm = _proj(kind, xp, norm_mix[i], w_in)
            ks, vs, qs_pm, ks_pm, vs_pm, _ = _proj(kind, xs, norm_mix[i], w_in)
            op_pm = _sb_flash(batch, seq, qp_pm, kp_pm, vtp_pm)
            w_out = c_w_out[slot]

        q64 = _from_pairs(qs_pm).reshape(n_seq, N_ROWS, HEAD_DIM)
        knew = _new_chunk_rows(_from_pairs(ks_pm), n_seq, dec_seq)
        vnew = _new_chunk_rows(_from_pairs(vs_pm), n_seq, dec_seq)
        o64 = _sample_attn(kind, i, page_table, cache_k2, cache_v2, q64, knew, vnew,
                           bias=bias_s, dsa=dsa)
        os_pm = _to_pairs(o64.reshape(n_seq * dec_seq, HHD)).astype(BF16)

        w_out = w_out.astype(BF16)
        xp = _outproj(xp, op_pm, w_out)
        xs = _outproj(xs, os_pm, w_out)
        xp = _ffn(xp, norm_ffn2[i], wgu2, wd2)
        xs = _ffn(xs, norm_ffn2[i], wgu2, wd2)

        new_k_p.append(kp.reshape(batch, seq, N_HEADS, HEAD_DIM))
        new_v_p.append(vp.reshape(batch, seq, N_HEADS, HEAD_DIM))
        new_k_s.append(ks.reshape(n_seq, dec_seq, N_HEADS, HEAD_DIM))
        new_v_s.append(vs.reshape(n_seq, dec_seq, N_HEADS, HEAD_DIM))

    return (xp.reshape(batch, seq, D_MODEL), xs.reshape(n_seq, dec_seq, D_MODEL),
            jnp.stack(new_k_p), jnp.stack(new_v_p), jnp.stack(new_ki_p),
            jnp.stack(new_k_s), jnp.stack(new_v_s), jnp.stack(new_ki_s))
```

```python
import functools
import math

import jax
import jax.numpy as jnp
from jax import lax
from jax.experimental import pallas as pl
from jax.experimental.pallas import tpu as pltpu

F32 = jnp.float32
BF16 = jnp.bfloat16
I32 = jnp.int32

D_MODEL = 1024
N_HEADS = 16
HEAD_DIM = 64
HHD = N_HEADS * HEAD_DIM
LANES = 128
N_PAIRS = HHD // LANES
D_FF = 2816
FF_CHUNK = 256
N_IDX_HEADS = 16
D_IDX = 64
DSA_TOPK = 256
MOBA_BLOCK = 256
MOBA_TOPK = 3
N_BUCKETS = 32
MAX_EXACT = 16
MAX_DISTANCE = 128
PAGE_SIZE = 128
QB = 256
CH = 256
EPS = 1e-6
NEG = -1e30
INT_MIN = -(2 ** 31)
EXP_ZERO_BELOW = -104.0
SCALE = HEAD_DIM ** -0.5
IDX_SCALE = (N_IDX_HEADS * D_IDX) ** -0.5
VMEM_LIMIT = 56 * 1024 * 1024
TOKEN_TILE = 512

KIND_DSA, KIND_MOBA, KIND_SB = 0, 1, 2


def _cparams(*semantics):
    return pltpu.CompilerParams(dimension_semantics=semantics, vmem_limit_bytes=VMEM_LIMIT)


def _dot(a, b):
    return jnp.dot(a, b, preferred_element_type=F32)


def _dot_t(a, b):
    return lax.dot_general(a, b, (((1,), (1,)), ((), ())), preferred_element_type=F32)


def _split_hi_lo(v):
    hi = v.astype(BF16)
    lo = (v - hi.astype(F32)).astype(BF16)
    return hi, lo


def _rms(x, g):
    ms = jnp.mean(x * x, axis=-1, keepdims=True)
    return x * lax.rsqrt(ms + EPS) * g


def _softplus(z):
    return jnp.maximum(z, 0.0) + jnp.log(1.0 + jnp.exp(-jnp.abs(z)))


def _ffn_body(x_ref, g_ref, wgu_ref, wd_ref, o_ref):
    x = x_ref[...]
    xn = _rms(x, g_ref[...]).astype(BF16)
    acc = jnp.zeros_like(x)
    for c in range(D_FF // FF_CHUNK):
        lo = c * FF_CHUNK
        a = _dot(xn, wgu_ref[:, lo:lo + FF_CHUNK])
        b = _dot(xn, wgu_ref[:, D_FF + lo:D_FF + lo + FF_CHUNK])
        h = (a * (1.0 / (1.0 + jnp.exp(-a))) * b).astype(BF16)
        acc = acc + _dot(h, wd_ref[lo:lo + FF_CHUNK, :])
    o_ref[...] = x + 0.5 * acc


def _ffn(x, g, wgu, wd):
    n = x.shape[0]
    tm = min(TOKEN_TILE, n)
    return pl.pallas_call(
        _ffn_body,
        grid=(n // tm,),
        in_specs=[pl.BlockSpec((tm, D_MODEL), lambda i: (i, 0)),
                  pl.BlockSpec((1, D_MODEL), lambda i: (0, 0)),
                  pl.BlockSpec((D_MODEL, 2 * D_FF), lambda i: (0, 0)),
                  pl.BlockSpec((D_FF, D_MODEL), lambda i: (0, 0))],
        out_specs=pl.BlockSpec((tm, D_MODEL), lambda i: (i, 0)),
        out_shape=jax.ShapeDtypeStruct(x.shape, F32),
        compiler_params=_cparams("parallel"),
        name="ffn",
    )(x, g.reshape(1, D_MODEL), wgu, wd)


def _head_rms(y, g_row):
    r_i = lax.broadcasted_iota(I32, (HHD, LANES), 0) >> 6
    c_i = lax.broadcasted_iota(I32, (HHD, LANES), 1)
    gather = jnp.where(r_i == c_i, 1.0, 0.0).astype(BF16)
    r2 = lax.broadcasted_iota(I32, (LANES, HHD), 0)
    c2 = lax.broadcasted_iota(I32, (LANES, HHD), 1) >> 6
    spread = jnp.where(r2 == c2, 1.0, 0.0).astype(BF16)
    hi, lo = _split_hi_lo(y * y)
    ssq = _dot(hi, gather) + _dot(lo, gather)
    r = lax.rsqrt(ssq * (1.0 / HEAD_DIM) + EPS)
    rhi, rlo = _split_hi_lo(r)
    rfull = _dot(rhi, spread) + _dot(rlo, spread)
    return y * rfull * g_row


def _store_pairs(ref, y):
    for p in range(N_PAIRS):
        ref[p] = y[:, p * LANES:(p + 1) * LANES].astype(BF16)


def _proj_body(kind, with_kmean, *refs):
    if kind == KIND_DSA:
        (x_ref, g_ref, w_ref, qg_ref, kg_ref, wt_ref, kig_ref,
         k_out, v_out, q_pm, k_pm, v_pm, vt_pm, qi_pm, ki_out, ki2_out, wi_out) = refs
    elif kind == KIND_MOBA:
        x_ref, g_ref, w_ref, qg_ref, kg_ref, k_out, v_out, q_pm, k_pm, v_pm, vt_pm = refs[:11]
        kmean_out = refs[11] if with_kmean else None
    else:
        x_ref, g_ref, w_ref, k_out, v_out, q_pm, k_pm, v_pm, vt_pm = refs
    xn = _rms(x_ref[...], g_ref[...]).astype(BF16)
    q = _dot(xn, w_ref[:, 0:HHD])
    k = _dot(xn, w_ref[:, HHD:2 * HHD])
    v = _dot(xn, w_ref[:, 2 * HHD:3 * HHD])
    if kind != KIND_SB:
        q = _head_rms(q, qg_ref[...])
        k = _head_rms(k, kg_ref[...])
    k_out[...] = k
    v_out[...] = v
    _store_pairs(q_pm, q * SCALE)
    _store_pairs(k_pm, k)
    _store_pairs(v_pm, v)
    for p in range(N_PAIRS):
        vt_pm[p] = v[:, p * LANES:(p + 1) * LANES].T.astype(BF16)
    if kind == KIND_DSA:
        _store_pairs(qi_pm, _dot(xn, w_ref[:, 3 * HHD:4 * HHD]))
        ki2 = _dot(xn, wt_ref[:, 0:LANES])
        ki2 = ki2 * lax.rsqrt(jnp.mean(ki2 * ki2, axis=-1, keepdims=True) + EPS) * kig_ref[...]
        ki_out[...] = ki2[:, 0:D_IDX]
        ki2_out[...] = ki2.astype(BF16)
        wi_out[...] = _dot(xn, wt_ref[:, LANES:2 * LANES])[:, 0:N_IDX_HEADS] * IDX_SCALE
    if kind == KIND_MOBA and with_kmean:
        tm = k.shape[0]
        for j in range(tm // MOBA_BLOCK):
            kmean_out[j] = jnp.mean(k[j * MOBA_BLOCK:(j + 1) * MOBA_BLOCK], axis=0, keepdims=True)


def _proj(kind, x, g, w, qg=None, kg=None, w_tail=None, kig=None, with_kmean=False):
    n = x.shape[0]
    tm = min(TOKEN_TILE, n)
    row = lambda i: (i, 0)
    fixed = lambda i: (0, 0)
    pm_spec = pl.BlockSpec((N_PAIRS, tm, LANES), lambda i: (0, i, 0))
    pm_shape = jax.ShapeDtypeStruct((N_PAIRS, n, LANES), BF16)
    vt_spec = pl.BlockSpec((N_PAIRS, LANES, tm), lambda i: (0, 0, i))
    vt_shape = jax.ShapeDtypeStruct((N_PAIRS, LANES, n), BF16)
    args = [x, g.reshape(1, D_MODEL), w]
    in_specs = [pl.BlockSpec((tm, D_MODEL), row), pl.BlockSpec((1, D_MODEL), fixed),
                pl.BlockSpec(w.shape, fixed)]
    if kind != KIND_SB:
        args += [jnp.tile(qg, N_HEADS).reshape(1, HHD), jnp.tile(kg, N_HEADS).reshape(1, HHD)]
        in_specs += [pl.BlockSpec((1, HHD), fixed)] * 2
    if kind == KIND_DSA:
        args += [w_tail, jnp.tile(kig, LANES // D_IDX).reshape(1, LANES)]
        in_specs += [pl.BlockSpec(w_tail.shape, fixed), pl.BlockSpec((1, LANES), fixed)]
    out_shape = [jax.ShapeDtypeStruct((n, HHD), F32)] * 2 + [pm_shape] * 3 + [vt_shape]
    out_specs = [pl.BlockSpec((tm, HHD), row)] * 2 + [pm_spec] * 3 + [vt_spec]
    if kind == KIND_DSA:
        out_shape += [pm_shape, jax.ShapeDtypeStruct((n, D_IDX), F32),
                      jax.ShapeDtypeStruct((n, LANES), BF16),
                      jax.ShapeDtypeStruct((n, N_IDX_HEADS), F32)]
        out_specs += [pm_spec, pl.BlockSpec((tm, D_IDX), row), pl.BlockSpec((tm, LANES), row),
                      pl.BlockSpec((tm, N_IDX_HEADS), row)]
    if kind == KIND_MOBA and with_kmean:
        out_shape += [jax.ShapeDtypeStruct((n // MOBA_BLOCK, 1, HHD), F32)]
        out_specs += [pl.BlockSpec((tm // MOBA_BLOCK, 1, HHD), lambda i: (i, 0, 0))]
    return pl.pallas_call(
        functools.partial(_proj_body, kind, with_kmean),
        grid=(n // tm,), in_specs=in_specs, out_specs=out_specs, out_shape=out_shape,
        compiler_params=_cparams("parallel"), name="proj",
    )(*args)


def _outproj_body(x_ref, o_ref, w_ref, y_ref):
    o = jnp.concatenate([o_ref[p] for p in range(N_PAIRS)], axis=1)
    y_ref[...] = x_ref[...] + _dot(o, w_ref[...])


def _outproj(x, o_pm, w):
    n = x.shape[0]
    tm = min(TOKEN_TILE, n)
    return pl.pallas_call(
        _outproj_body,
        grid=(n // tm,),
        in_specs=[pl.BlockSpec((tm, D_MODEL), lambda i: (i, 0)),
                  pl.BlockSpec((N_PAIRS, tm, LANES), lambda i: (0, i, 0)),
                  pl.BlockSpec((HHD, D_MODEL), lambda i: (0, 0))],
        out_specs=pl.BlockSpec((tm, D_MODEL), lambda i: (i, 0)),
        out_shape=jax.ShapeDtypeStruct(x.shape, F32),
        compiler_params=_cparams("parallel"), name="outproj",
    )(x, o_pm, w)


def _sortable_key(score):
    bits = pltpu.bitcast(score + 0.0, I32)
    return bits ^ ((bits >> 31) & 0x7FFFFFFF)


def _topk_threshold(count_ge, shape, topk):
    zero = jnp.zeros(shape, I32)
    t0 = jnp.where(count_ge(zero) >= topk, zero, jnp.full(shape, INT_MIN, I32))

    def bit_step(i, t):
        cand = t + jnp.left_shift(jnp.int32(1), jnp.int32(30) - i)
        return jnp.where(count_ge(cand) >= topk, cand, t)

    return lax.fori_loop(0, 31, bit_step, t0)


def _top3_flags(gate, axis):
    idx = lax.broadcasted_iota(I32, gate.shape, axis)
    sel = jnp.zeros(gate.shape, F32)
    for _ in range(MOBA_TOPK):
        mx = jnp.max(gate, axis=axis, keepdims=True)
        cand = jnp.where((gate == mx) & (gate > -jnp.inf), idx, gate.shape[axis])
        first = jnp.min(cand, axis=axis, keepdims=True)
        pick = idx == first
        sel = jnp.where(pick, 1.0, sel)
        gate = jnp.where(pick, -jnp.inf, gate)
    return sel


def _stack_masked_q(q2):
    lane = lax.broadcasted_iota(I32, q2.shape, 1)
    zero = jnp.zeros_like(q2)
    return jnp.concatenate([jnp.where(lane < HEAD_DIM, q2, zero),
                            jnp.where(lane >= HEAD_DIM, q2, zero)], axis=0)


def _pair_out(acc_p, scale_p=None):
    a, b = acc_p[0:HEAD_DIM, 0:QB], acc_p[HEAD_DIM:LANES, QB:2 * QB]
    if scale_p is not None:
        a, b = a * scale_p[:, 0:QB], b * scale_p[:, QB:2 * QB]
    return jnp.concatenate([a, b], axis=0).T.astype(BF16)


def _flash_step_t(s_t, m, l, acc, v_t):
    m_new = jnp.maximum(m, jnp.max(s_t, axis=0, keepdims=True))
    alpha = jnp.exp(m - m_new)
    p = jnp.exp(s_t - m_new)
    l = alpha * l + jnp.sum(p, axis=0, keepdims=True)
    acc = alpha * acc + _dot(v_t, p.astype(BF16))
    return m_new, l, acc


def _dsa_mask_body(topk, nq, qi_ref, ki2_ref, wit_ref, o_ref, key_ref, qim_ref):
    qb = pl.program_id(1)
    nck = qb + 1
    for p in range(N_PAIRS):
        qim_ref[p] = _stack_masked_q(qi_ref[p])
    wt = wit_ref[...]
    rowk = lax.broadcasted_iota(I32, (CH, QB), 0)
    laneq = lax.broadcasted_iota(I32, (CH, QB), 1)

    def score_chunk(c, carry):
        ki = ki2_ref[pl.ds(pl.multiple_of(c * CH, CH), CH), :]
        s = jnp.zeros((CH, QB), F32)
        for p in range(N_PAIRS):
            d = jnp.maximum(_dot_t(ki, qim_ref[p]), 0.0)
            s = s + wt[2 * p:2 * p + 1] * d[:, 0:QB] + wt[2 * p + 1:2 * p + 2] * d[:, QB:2 * QB]
        s = jnp.where(c * CH + rowk <= qb * QB + laneq, s, -jnp.inf)
        key_ref[c] = _sortable_key(s)
        return carry

    lax.fori_loop(0, nck, score_chunk, 0)

    def count_ge(cand):
        def body(c, acc):
            hit = jnp.where(key_ref[c] >= cand, 1.0, 0.0)
            return acc + jnp.sum(hit.reshape(CH // 8, 8, QB), axis=0)

        acc = lax.fori_loop(0, nck, body, jnp.zeros((8, QB), F32))
        return jnp.sum(acc, axis=0, keepdims=True)

    thr = _topk_threshold(count_ge, (1, QB), float(topk))

    def mask_chunk(c, carry):
        o_ref[pl.ds(pl.multiple_of(c * CH, CH), CH), :] = jnp.where(key_ref[c] >= thr, 0.0, NEG).astype(BF16)
        return carry

    lax.fori_loop(0, nck, mask_chunk, 0)

    def fill_chunk(c, carry):
        o_ref[pl.ds(pl.multiple_of(c * CH, CH), CH), :] = jnp.full((CH, QB), NEG, BF16)
        return carry

    lax.fori_loop(nck, nq, fill_chunk, 0)


def _dsa_mask(batch, seq, qi_pm, ki2, wi_t):
    nq = seq // QB
    topk = min(DSA_TOPK, seq // 4)
    return pl.pallas_call(
        functools.partial(_dsa_mask_body, topk, nq),
        grid=(batch, nq),
        in_specs=[pl.BlockSpec((N_PAIRS, QB, LANES), lambda b, i: (0, b * nq + i, 0)),
                  pl.BlockSpec((seq, LANES), lambda b, i: (b, 0)),
                  pl.BlockSpec((N_IDX_HEADS, QB), lambda b, i: (0, b * nq + i))],
        out_specs=pl.BlockSpec((None, seq, QB), lambda b, i: (b, 0, i)),
        out_shape=jax.ShapeDtypeStruct((batch, seq, seq), BF16),
        scratch_shapes=[pltpu.VMEM((nq, CH, QB), I32), pltpu.VMEM((N_PAIRS, 2 * QB, LANES), BF16)],
        compiler_params=_cparams("parallel", "parallel"), name="dsa_mask",
    )(qi_pm, ki2, wi_t)


def _masked_flash_body(kind, q_ref, k_ref, vt_ref, bias_ref, x_ref, o_ref,
                       qmm_ref, m_ref, l_ref, acc_ref, *sel):
    qb = pl.program_id(1)
    c = pl.program_id(2)

    @pl.when(c == 0)
    def _init():
        for p in range(N_PAIRS):
            qmm = _stack_masked_q(q_ref[p])
            qmm_ref[p] = qmm
            m_ref[p] = jnp.full((1, 2 * QB), NEG, F32)
            l_ref[p] = jnp.zeros((1, 2 * QB), F32)
            acc_ref[p] = jnp.zeros((LANES, 2 * QB), F32)
            if kind == KIND_MOBA:
                sel_ref = sel[0]
                n_rows = sel_ref.shape[1]
                gate = _dot_t(x_ref[p], qmm)[0:n_rows]
                blk = lax.broadcasted_iota(I32, gate.shape, 0)
                sel_ref[p] = _top3_flags(jnp.where(blk < qb, gate, -jnp.inf), 0)

    @pl.when(c <= qb)
    def _step():
        if kind == KIND_DSA:
            mask = x_ref[...].astype(F32)
            mask2 = jnp.concatenate([mask, mask], axis=1)
        for p in range(N_PAIRS):
            s_t = _dot_t(k_ref[p], qmm_ref[p]) + bias_ref[p]
            if kind == KIND_DSA:
                s_t = s_t + mask2
            else:
                vis = sel[0][p, pl.ds(c, 1), :] + jnp.where(c == qb, 1.0, 0.0)
                s_t = s_t + jnp.where(vis > 0.5, 0.0, NEG)
            m, l, acc = _flash_step_t(s_t, m_ref[p], l_ref[p], acc_ref[p], vt_ref[p])
            m_ref[p] = m
            l_ref[p] = l
            acc_ref[p] = acc

    @pl.when(c == qb)
    def _finish():
        for p in range(N_PAIRS):
            o_ref[p] = _pair_out(acc_ref[p], 1.0 / l_ref[p])


def _masked_flash(kind, batch, seq, q_pm, k_pm, vt_pm, bias, extra):
    nq = seq // QB
    q_spec = pl.BlockSpec((N_PAIRS, QB, LANES), lambda b, i, c: (0, b * nq + i, 0))
    in_specs = [q_spec,
                pl.BlockSpec((N_PAIRS, CH, LANES), lambda b, i, c: (0, b * nq + jnp.minimum(c, i), 0)),
                pl.BlockSpec((N_PAIRS, LANES, CH), lambda b, i, c: (0, 0, b * nq + jnp.minimum(c, i))),
                pl.BlockSpec((N_PAIRS, None, CH, 2 * QB),
                             lambda b, i, c: (0, jnp.clip(i - c, 0, 2), 0, 0))]
    scratch = [pltpu.VMEM((N_PAIRS, 2 * QB, LANES), BF16), pltpu.VMEM((N_PAIRS, 1, 2 * QB), F32),
               pltpu.VMEM((N_PAIRS, 1, 2 * QB), F32), pltpu.VMEM((N_PAIRS, LANES, 2 * QB), F32)]
    if kind == KIND_DSA:
        in_specs.append(pl.BlockSpec((None, CH, QB), lambda b, i, c: (b, jnp.minimum(c, i), i)))
    else:
        in_specs.append(pl.BlockSpec((N_PAIRS, None, LANES, LANES), lambda b, i, c: (0, b, 0, 0)))
        scratch.append(pltpu.VMEM((N_PAIRS, -(-nq // 8) * 8, 2 * QB), F32))
    return pl.pallas_call(
        functools.partial(_masked_flash_body, kind),
        grid=(batch, nq, nq), in_specs=in_specs, out_specs=q_spec,
        out_shape=jax.ShapeDtypeStruct((N_PAIRS, batch * seq, LANES), BF16),
        scratch_shapes=scratch,
        compiler_params=_cparams("parallel", "parallel", "arbitrary"), name="masked_flash",
    )(q_pm, k_pm, vt_pm, bias, extra)


def _sb_flash_body(q_ref, k_ref, vt_ref, o_ref, qmm_ref, tail_ref, acc_ref, live_ref):
    qb = pl.program_id(1)
    c = pl.program_id(2)

    @pl.when(c == 0)
    def _init():
        for p in range(N_PAIRS):
            qmm_ref[p] = _stack_masked_q(q_ref[p])
            tail_ref[p] = jnp.zeros((1, 2 * QB), F32)
            acc_ref[p] = jnp.zeros((LANES, 2 * QB), F32)
        live_ref[0] = 1

    @pl.when((c <= qb) & (live_ref[0] == 1))
    def _step():
        rowk = lax.broadcasted_iota(I32, (CH, 2 * QB), 0)
        laneq = lax.broadcasted_iota(I32, (CH, 2 * QB), 1) & (QB - 1)
        strict = (qb - c) * CH + rowk < qb * QB + laneq
        later = (lax.broadcasted_iota(I32, (CH, CH), 1) > lax.broadcasted_iota(I32, (CH, CH), 0))
        later = jnp.where(later, 1.0, 0.0).astype(BF16)
        worst = jnp.full((1, 1), -jnp.inf, F32)
        for p in range(N_PAIRS):
            z = _dot_t(k_ref[p], qmm_ref[p])
            sp = _softplus(z)
            lr = jnp.where(strict, -sp, 0.0)
            hi, lo = _split_hi_lo(lr)
            tail = tail_ref[p]
            a = jnp.where(strict, jnp.exp(z - sp + tail + _dot(later, hi) + _dot(later, lo)), 0.0)
            acc_ref[p] = acc_ref[p] + _dot(vt_ref[p], a.astype(BF16))
            tail = tail + jnp.sum(lr, axis=0, keepdims=True)
            tail_ref[p] = tail
            worst = jnp.maximum(worst, jnp.max(tail, axis=1, keepdims=True))
        live_ref[0] = jnp.where(worst[0, 0] > EXP_ZERO_BELOW, 1, 0)

    @pl.when(c == qb)
    def _finish():
        for p in range(N_PAIRS):
            o_ref[p] = _pair_out(acc_ref[p])


def _sb_flash(batch, seq, q_pm, k_pm, vt_pm):
    nq = seq // QB
    q_spec = pl.BlockSpec((N_PAIRS, QB, LANES), lambda b, i, c: (0, b * nq + i, 0))
    return pl.pallas_call(
        _sb_flash_body,
        grid=(batch, nq, nq),
        in_specs=[q_spec,
                  pl.BlockSpec((N_PAIRS, CH, LANES), lambda b, i, c: (0, b * nq + jnp.maximum(i - c, 0), 0)),
                  pl.BlockSpec((N_PAIRS, LANES, CH), lambda b, i, c: (0, 0, b * nq + jnp.maximum(i - c, 0)))],
        out_specs=q_spec,
        out_shape=jax.ShapeDtypeStruct((N_PAIRS, batch * seq, LANES), BF16),
        scratch_shapes=[pltpu.VMEM((N_PAIRS, 2 * QB, LANES), BF16), pltpu.VMEM((N_PAIRS, 1, 2 * QB), F32),
                        pltpu.VMEM((N_PAIRS, LANES, 2 * QB), F32), pltpu.SMEM((1,), I32)],
        compiler_params=_cparams("parallel", "parallel", "arbitrary"), name="sb_flash",
    )(q_pm, k_pm, vt_pm)


N_ROWS = 64
PAGES_PER_STEP = 4
PAGES_PER_BLOCK = MOBA_BLOCK // PAGE_SIZE


def _softmax_update(s, v_t, m_ref, l_ref, acc_ref):
    m_old = m_ref[...]
    m_new = jnp.maximum(m_old, jnp.max(s, axis=1, keepdims=True))
    alpha = jnp.exp(m_old - m_new)
    p = jnp.exp(s - m_new)
    m_ref[...] = m_new
    l_ref[...] = alpha * l_ref[...] + jnp.sum(p, axis=1, keepdims=True)
    acc_ref[...] = alpha * acc_ref[...] + _dot_t(p.astype(BF16), v_t)


def _dec_init(m_ref, l_ref, acc_ref):
    m_ref[...] = jnp.full((N_ROWS, 1), NEG, F32)
    l_ref[...] = jnp.zeros((N_ROWS, 1), F32)
    acc_ref[...] = jnp.zeros(acc_ref.shape, F32)


def _rows_from_tokens(x8, dec_seq):
    return jnp.concatenate([jnp.broadcast_to(x8[t:t + 1], (N_HEADS, x8.shape[1]))
                            for t in range(dec_seq)], axis=0)


def _lane_concat_bf16(page_refs):
    return jnp.concatenate([ref[...].astype(BF16) for ref in page_refs], axis=1)


def _page_bias(bias_ref, pg, n_pages):
    return bias_ref[jnp.where(pg == n_pages - 1, 1, 0)]


def _dsa_sample_body(n_pages, dec_seq, topk, pt_ref, q_ref, qi_ref, wib_ref, *refs):
    k_pages = refs[0:PAGES_PER_STEP]
    v_pages = refs[PAGES_PER_STEP:2 * PAGES_PER_STEP]
    ki_pages = refs[2 * PAGES_PER_STEP:2 * PAGES_PER_STEP + n_pages]
    (knew_ref, vnew_ref, kinew_ref, bias_ref, o_ref,
     mb_ref, m_ref, l_ref, acc_ref) = refs[2 * PAGES_PER_STEP + n_pages:]
    j = pl.program_id(1)
    last = n_pages // PAGES_PER_STEP - 1
    q = q_ref[...]

    @pl.when(j == 0)
    def _select():
        qi = qi_ref[...]
        wib = wib_ref[...]
        rows8 = lax.broadcasted_iota(I32, (8, PAGE_SIZE), 0)
        lane8 = lax.broadcasted_iota(I32, (8, PAGE_SIZE), 1)
        keys = []
        for pg in range(n_pages + 1):
            ki_t = ki_pages[pg][...].astype(BF16) if pg < n_pages else kinew_ref[...]
            wr = wib * jnp.maximum(_dot(qi, ki_t), 0.0)
            per_t = [jnp.sum(wr[t * N_HEADS:(t + 1) * N_HEADS], axis=0, keepdims=True)
                     for t in range(dec_seq)]
            sc = jnp.concatenate(per_t + [jnp.zeros((8 - dec_seq, PAGE_SIZE), F32)], axis=0)
            valid = rows8 < dec_seq
            if pg == n_pages:
                valid = valid & (lane8 <= rows8)
            keys.append(_sortable_key(jnp.where(valid, sc, -jnp.inf)))

        def count_ge(cand):
            acc = jnp.zeros((8, PAGE_SIZE), F32)
            for kk in keys:
                acc = acc + jnp.where(kk >= cand, 1.0, 0.0)
            return jnp.sum(acc, axis=1, keepdims=True)

        thr = _topk_threshold(count_ge, (8, 1), float(topk))
        for pg in range(n_pages + 1):
            mb_ref[pg] = jnp.where(keys[pg] >= thr, 0.0, NEG)
        _dec_init(m_ref, l_ref, acc_ref)

    first = PAGES_PER_STEP * j
    add = jnp.concatenate([_page_bias(bias_ref, first + r, n_pages)
                           + _rows_from_tokens(mb_ref[first + r], dec_seq)
                           for r in range(PAGES_PER_STEP)], axis=1)
    s = _dot(q, _lane_concat_bf16(k_pages)) + add
    _softmax_update(s, _lane_concat_bf16(v_pages), m_ref, l_ref, acc_ref)

    @pl.when(j == last)
    def _finish():
        s = _dot(q, knew_ref[...]) + bias_ref[2] + _rows_from_tokens(mb_ref[n_pages], dec_seq)
        _softmax_update(s, vnew_ref[...], m_ref, l_ref, acc_ref)
        o_ref[...] = acc_ref[...] / l_ref[...]


def _moba_sample_body(n_pages, pt_ref, q_ref, *refs):
    k_pages = refs[0:PAGES_PER_STEP]
    v_pages = refs[PAGES_PER_STEP:2 * PAGES_PER_STEP]
    (knew_ref, vnew_ref, bias_ref, o_ref,
     gate_ref, pm_ref, pl_ref, pacc_ref, m_ref, l_ref, acc_ref) = refs[2 * PAGES_PER_STEP:]
    j = pl.program_id(1)
    n_blk = n_pages // PAGES_PER_BLOCK
    last = n_pages // PAGES_PER_STEP - 1
    q = q_ref[...]
    lane = lax.broadcasted_iota(I32, (N_ROWS, LANES), 1)

    @pl.when(j == 0)
    def _zero():
        gate_ref[...] = jnp.zeros((N_ROWS, LANES), F32)

    for bi in range(PAGES_PER_STEP // PAGES_PER_BLOCK):
        blk = (PAGES_PER_STEP // PAGES_PER_BLOCK) * j + bi
        _dec_init(m_ref, l_ref, acc_ref)
        rs = range(bi * PAGES_PER_BLOCK, (bi + 1) * PAGES_PER_BLOCK)
        raw = _dot(q, _lane_concat_bf16([k_pages[r] for r in rs]))
        logit_sum = jnp.sum(raw, axis=1, keepdims=True)
        add = jnp.concatenate([_page_bias(bias_ref, PAGES_PER_STEP * j + r, n_pages) for r in rs], axis=1)
        _softmax_update(raw + add, _lane_concat_bf16([v_pages[r] for r in rs]), m_ref, l_ref, acc_ref)
        gate_ref[...] = gate_ref[...] + jnp.where(lane == blk, logit_sum * (1.0 / MOBA_BLOCK), 0.0)
        pm_ref[blk] = m_ref[...]
        pl_ref[blk] = l_ref[...]
        pacc_ref[blk] = acc_ref[...]

    @pl.when(j == last)
    def _finish():
        sel = _top3_flags(jnp.where(lane < n_blk, gate_ref[...], -jnp.inf), 1)
        _dec_init(m_ref, l_ref, acc_ref)
        _softmax_update(_dot(q, knew_ref[...]) + bias_ref[2], vnew_ref[...], m_ref, l_ref, acc_ref)
        picked = [sel[:, n:n + 1] > 0.5 for n in range(n_blk)]
        m_all = m_ref[...]
        for n in range(n_blk):
            m_all = jnp.maximum(m_all, jnp.where(picked[n], pm_ref[n], NEG))
        w_own = jnp.exp(m_ref[...] - m_all)
        l_all = w_own * l_ref[...]
        acc = w_own * acc_ref[...]
        for n in range(n_blk):
            w = jnp.where(picked[n], jnp.exp(pm_ref[n] - m_all), 0.0)
            l_all = l_all + w * pl_ref[n]
            acc = acc + w * pacc_ref[n]
        o_ref[...] = acc / l_all


def _sb_sample_body(n_pages, dec_seq, pt_ref, q_ref, *refs):
    k_pages = refs[0:PAGES_PER_STEP]
    v_pages = refs[PAGES_PER_STEP:2 * PAGES_PER_STEP]
    knew_ref, vnew_ref, o_ref, tail_ref, acc_ref, live_ref = refs[2 * PAGES_PER_STEP:]
    j = pl.program_id(1)
    q = q_ref[...]
    n = PAGE_SIZE

    def chunk(z, strict, v_t):
        jj = lax.broadcasted_iota(I32, (n, 2 * n), 0)
        ss = lax.broadcasted_iota(I32, (n, 2 * n), 1)
        ue = jnp.where((jj > ss) | (ss >= n), 1.0, 0.0).astype(BF16)
        sp = _softplus(z)
        lr = -sp if strict is None else jnp.where(strict, -sp, 0.0)
        hi, lo = _split_hi_lo(lr)
        cs = _dot(hi, ue) + _dot(lo, ue)
        a = jnp.exp(z - sp + tail_ref[...] + cs[:, :n])
        if strict is not None:
            a = jnp.where(strict, a, 0.0)
        acc_ref[...] = acc_ref[...] + _dot_t(a.astype(BF16), v_t)
        tail_ref[...] = tail_ref[...] + cs[:, n:]

    @pl.when(j == 0)
    def _newest():
        tail_ref[...] = jnp.zeros((N_ROWS, n), F32)
        acc_ref[...] = jnp.zeros((N_ROWS, HHD), F32)
        live_ref[0] = 1
        tok = lax.broadcasted_iota(I32, (N_ROWS, n), 0) // N_HEADS
        col = lax.broadcasted_iota(I32, (N_ROWS, n), 1)
        chunk(_dot(q, knew_ref[...]), col < tok, vnew_ref[...])

    @pl.when(live_ref[0] == 1)
    def _pages():
        for r in reversed(range(PAGES_PER_STEP)):
            chunk(_dot(q, k_pages[r][...].astype(BF16)), None, v_pages[r][...].astype(BF16))
        worst = jnp.max(jnp.max(tail_ref[...], axis=1, keepdims=True), axis=0, keepdims=True)
        live_ref[0] = jnp.where(worst[0, 0] > EXP_ZERO_BELOW, 1, 0)

    @pl.when(j == n_pages // PAGES_PER_STEP - 1)
    def _finish():
        o_ref[...] = acc_ref[...]


def _sample_attn(kind, layer, page_table, cache_kt, cache_vt, qbd, knew_t, vnew_t, bias=None, dsa=None):
    n_seq, n_pages = page_table.shape
    n_steps = n_pages // PAGES_PER_STEP
    dec_seq = N_ROWS // N_HEADS

    def seq_spec(shape):
        return pl.BlockSpec((None,) + tuple(shape), lambda b, j, pt: (b,) + (0,) * len(shape))

    def page_spec(lyr, n_rows, page_of):
        return pl.BlockSpec((None, None, n_rows, PAGE_SIZE),
                            lambda b, j, pt: (lyr, pt[b, page_of(j)], 0, 0))

    if kind == KIND_SB:
        first = lambda j: n_pages - PAGES_PER_STEP * (j + 1)
    else:
        first = lambda j: PAGES_PER_STEP * j
    group = [page_spec(layer, HHD, functools.partial(lambda j, r: first(j) + r, r=r))
             for r in range(PAGES_PER_STEP)]
    args = [qbd]
    in_specs = [seq_spec((N_ROWS, HHD))]
    if kind == KIND_DSA:
        qi_r, wib, cache_kit, slot, kinew_t = dsa
        args += [qi_r, wib]
        in_specs += [seq_spec((N_ROWS, D_IDX)), seq_spec((N_ROWS, PAGE_SIZE))]
    args += [cache_kt] * PAGES_PER_STEP + [cache_vt] * PAGES_PER_STEP
    in_specs += group * 2
    if kind == KIND_DSA:
        args += [cache_kit] * n_pages
        in_specs += [page_spec(slot, D_IDX, functools.partial(lambda j, pg: pg, pg=pg))
                     for pg in range(n_pages)]
    args += [knew_t, vnew_t]
    in_specs += [seq_spec((HHD, PAGE_SIZE))] * 2
    if kind == KIND_DSA:
        args += [kinew_t]
        in_specs += [seq_spec((D_IDX, PAGE_SIZE))]
    if kind != KIND_SB:
        args += [bias]
        in_specs += [pl.BlockSpec((3, N_ROWS, PAGE_SIZE), lambda b, j, pt: (0, 0, 0))]
    acc_shape = pltpu.VMEM((N_ROWS, HHD), F32)
    stat_shape = pltpu.VMEM((N_ROWS, 1), F32)
    if kind == KIND_DSA:
        topk = min(DSA_TOPK, (n_pages * PAGE_SIZE + dec_seq) // 4)
        body = functools.partial(_dsa_sample_body, n_pages, dec_seq, topk)
        scratch = [pltpu.VMEM((n_pages + 1, 8, PAGE_SIZE), F32), stat_shape, stat_shape, acc_shape]
    elif kind == KIND_MOBA:
        n_blk = n_pages // PAGES_PER_BLOCK
        body = functools.partial(_moba_sample_body, n_pages)
        scratch = [pltpu.VMEM((N_ROWS, LANES), F32),
                   pltpu.VMEM((n_blk, N_ROWS, 1), F32), pltpu.VMEM((n_blk, N_ROWS, 1), F32),
                   pltpu.VMEM((n_blk, N_ROWS, HHD), F32), stat_shape, stat_shape, acc_shape]
    else:
        body = functools.partial(_sb_sample_body, n_pages, dec_seq)
        scratch = [pltpu.VMEM((N_ROWS, PAGE_SIZE), F32), acc_shape, pltpu.SMEM((1,), I32)]
    return pl.pallas_call(
        body,
        grid_spec=pltpu.PrefetchScalarGridSpec(
            num_scalar_prefetch=1, grid=(n_seq, n_steps), in_specs=in_specs,
            out_specs=seq_spec((N_ROWS, HHD)), scratch_shapes=scratch),
        out_shape=jax.ShapeDtypeStruct((n_seq, N_ROWS, HHD), F32),
        compiler_params=_cparams("parallel", "arbitrary"), name="sample_attn",
    )(page_table, *args)


def _rel_bucket(dist):
    n = jnp.maximum(dist, 0)
    nf = jnp.maximum(n, 1).astype(F32)
    large = MAX_EXACT + (jnp.log(nf / MAX_EXACT) / math.log(MAX_DISTANCE / MAX_EXACT)
                         * (N_BUCKETS - MAX_EXACT)).astype(I32)
    large = jnp.minimum(large, N_BUCKETS - 1)
    return jnp.where(n < MAX_EXACT, n, large)


def _bias_lookup(rel_table, dist, shift=None):
    b = jnp.moveaxis(rel_table[_rel_bucket(dist)], -1, 0)
    if shift is not None:
        b = b - shift.reshape((N_HEADS,) + (1,) * dist.ndim)
    return jnp.where(dist[None] >= 0, b, NEG)


def _prompt_bias_tiles(rel_table):
    far = rel_table[_rel_bucket(jnp.int32(2 * QB))]
    j = jnp.arange(CH, dtype=I32)[:, None]
    i = jnp.arange(QB, dtype=I32)[None, :]
    near = [_bias_lookup(rel_table, u * CH + i - j, far) for u in range(2)]
    tiles = jnp.stack(near + [jnp.zeros_like(near[0])], axis=1)
    tiles = tiles.reshape(N_PAIRS, 2, 3, CH, QB)
    return jnp.concatenate([tiles[:, 0], tiles[:, 1]], axis=-1).astype(F32)


def _sample_bias_tiles(rel_table, dec_seq):
    t = jnp.arange(dec_seq, dtype=I32)[:, None]
    j = jnp.arange(PAGE_SIZE, dtype=I32)[None, :]
    far = _bias_lookup(rel_table, jnp.full((dec_seq, PAGE_SIZE), 2 * PAGE_SIZE, I32))
    last = _bias_lookup(rel_table, PAGE_SIZE + t - j)
    new = _bias_lookup(rel_table, jnp.where(j < dec_seq, t - j, -1))
    tiles = jnp.stack([far, last, new], axis=0)
    return jnp.transpose(tiles, (0, 2, 1, 3)).reshape(3, dec_seq * N_HEADS, PAGE_SIZE).astype(F32)


def _from_pairs(y_pm):
    return jnp.transpose(y_pm, (1, 0, 2)).reshape(y_pm.shape[1], HHD)


def _to_pairs(y):
    return jnp.transpose(y.reshape(y.shape[0], N_PAIRS, LANES), (1, 0, 2))


def _block_diag_rows(q, n_seq, dec_seq):
    q5 = q.reshape(n_seq, dec_seq, 1, N_HEADS, HEAD_DIM)
    eye = jnp.eye(N_HEADS, dtype=q.dtype).reshape(1, 1, N_HEADS, N_HEADS, 1)
    return (q5 * eye).reshape(n_seq, dec_seq * N_HEADS, HHD)


def _diag_heads(o_bd, n_seq, dec_seq):
    o6 = o_bd.reshape(n_seq, dec_seq, N_HEADS, N_HEADS, HEAD_DIM)
    hh = jnp.arange(N_HEADS)
    return o6[:, :, hh, hh, :].reshape(n_seq * dec_seq, HHD)


def _new_chunk_t(y, n_seq, dec_seq):
    y3 = jnp.transpose(y.reshape(n_seq, dec_seq, y.shape[-1]), (0, 2, 1))
    return jnp.pad(y3, ((0, 0), (0, 0), (0, PAGE_SIZE - dec_seq)))


def kernel(x_prompt, x_sample, cache_k, cache_v, cache_kidx, page_table, rel_table,
           norm_ffn1, ffn1_w_gu, ffn1_w_down, norm_mix, norm_ffn2, ffn2_w_gu, ffn2_w_down,
           a_w_in, a_w_out, a_q_norm, a_k_norm, a_kidx_norm,
           b_w_in, b_w_out, b_q_norm, b_k_norm, c_w_in, c_w_out):
    batch, seq, _ = x_prompt.shape
    n_seq, dec_seq, _ = x_sample.shape
    depth = cache_k.shape[0]
    n_pool = cache_k.shape[1]
    n_pages = page_table.shape[1]
    assert seq % TOKEN_TILE == 0 and QB == CH == MOBA_BLOCK and seq % QB == 0
    assert dec_seq * N_HEADS == N_ROWS and n_pages % PAGES_PER_STEP == 0
    assert PAGES_PER_STEP % PAGES_PER_BLOCK == 0 and cache_k.shape[2] == PAGE_SIZE

    xp = x_prompt.reshape(batch * seq, D_MODEL)
    xs = x_sample.reshape(n_seq * dec_seq, D_MODEL)
    cache_kt = jnp.transpose(cache_k, (0, 1, 3, 4, 2)).reshape(depth, n_pool, HHD, PAGE_SIZE)
    cache_vt = jnp.transpose(cache_v, (0, 1, 3, 4, 2)).reshape(depth, n_pool, HHD, PAGE_SIZE)
    cache_kit = jnp.transpose(cache_kidx, (0, 1, 3, 2))
    bias_p = _prompt_bias_tiles(rel_table)
    bias_s = _sample_bias_tiles(rel_table, dec_seq)
    n_blk = seq // MOBA_BLOCK

    new_k_p, new_v_p, new_ki_p, new_k_s, new_v_s, new_ki_s = [], [], [], [], [], []
    counts = [0, 0, 0]
    for i in range(depth):
        kind = i % 3
        slot = counts[kind]
        counts[kind] += 1
        wgu1, wd1 = ffn1_w_gu[i].astype(BF16), ffn1_w_down[i].astype(BF16)
        wgu2, wd2 = ffn2_w_gu[i].astype(BF16), ffn2_w_down[i].astype(BF16)
        xp = _ffn(xp, norm_ffn1[i], wgu1, wd1)
        xs = _ffn(xs, norm_ffn1[i], wgu1, wd1)

        dsa = None
        if kind == KIND_DSA:
            w_in = a_w_in[slot]
            w_main = w_in[:, :4 * HHD].astype(BF16)
            w_ki = w_in[:, 4 * HHD:4 * HHD + D_IDX]
            w_wi = jnp.pad(w_in[:, 4 * HHD + D_IDX:], ((0, 0), (0, LANES - N_IDX_HEADS)))
            w_tail = jnp.concatenate([w_ki, w_ki, w_wi], axis=1).astype(BF16)
            kw = dict(qg=a_q_norm[slot], kg=a_k_norm[slot], w_tail=w_tail, kig=a_kidx_norm[slot])
            kp, vp, qp_pm, kp_pm, _, vtp_pm, qip_pm, kip, ki2p, wip = _proj(kind, xp, norm_mix[i], w_main, **kw)
            ks, vs, qs_pm, ks_pm, vs_pm, _, qis_pm, kis, ki2s, wis = _proj(kind, xs, norm_mix[i], w_main, **kw)
            mask = _dsa_mask(batch, seq, qip_pm, ki2p, wip.T)
            op_pm = _masked_flash(kind, batch, seq, qp_pm, kp_pm, vtp_pm, bias_p, mask)
            qi_r = _from_pairs(qis_pm).reshape(n_seq, N_ROWS, D_IDX)
            wib = jnp.broadcast_to(wis.reshape(n_seq, N_ROWS, 1), (n_seq, N_ROWS, PAGE_SIZE))
            dsa = (qi_r, wib, cache_kit, slot, _new_chunk_t(ki2s[:, :D_IDX], n_seq, dec_seq))
            w_out = a_w_out[slot]
            new_ki_p.append(kip.reshape(batch, seq, D_IDX))
            new_ki_s.append(kis.reshape(n_seq, dec_seq, D_IDX))
        elif kind == KIND_MOBA:
            kw = dict(qg=b_q_norm[slot], kg=b_k_norm[slot])
            w_in = b_w_in[slot].astype(BF16)
            kp, vp, qp_pm, kp_pm, _, vtp_pm, kmean = _proj(kind, xp, norm_mix[i], w_in, with_kmean=True, **kw)
            ks, vs, qs_pm, ks_pm, vs_pm, _ = _proj(kind, xs, norm_mix[i], w_in, **kw)
            km = jnp.transpose(kmean.reshape(batch, n_blk, N_PAIRS, LANES), (2, 0, 1, 3))
            km = jnp.pad(km, ((0, 0), (0, 0), (0, LANES - n_blk), (0, 0))).astype(BF16)
            op_pm = _masked_flash(kind, batch, seq, qp_pm, kp_pm, vtp_pm, bias_p, km)
            w_out = b_w_out[slot]
        else:
            w_in = c_w_in[slot].astype(BF16)
            kp, vp, qp_pm, kp_pm, _, vtp_pm = _proj(kind, xp, norm_mix[i], w_in)
            ks, vs, qs_pm, ks_pm, vs_pm, _ = _proj(kind, xs, norm_mix[i], w_in)
            op_pm = _sb_flash(batch, seq, qp_pm, kp_pm, vtp_pm)
            w_out = c_w_out[slot]

        qbd = _block_diag_rows(_from_pairs(qs_pm), n_seq, dec_seq)
        knew_t = _new_chunk_t(_from_pairs(ks_pm), n_seq, dec_seq)
        vnew_t = _new_chunk_t(_from_pairs(vs_pm), n_seq, dec_seq)
        o_bd = _sample_attn(kind, i, page_table, cache_kt, cache_vt, qbd, knew_t, vnew_t,
                            bias=bias_s, dsa=dsa)
        os_pm = _to_pairs(_diag_heads(o_bd, n_seq, dec_seq)).astype(BF16)

        w_out = w_out.astype(BF16)
        xp = _outproj(xp, op_pm, w_out)
        xs = _outproj(xs, os_pm, w_out)
        xp = _ffn(xp, norm_ffn2[i], wgu2, wd2)
        xs = _ffn(xs, norm_ffn2[i], wgu2, wd2)

        new_k_p.append(kp.reshape(batch, seq, N_HEADS, HEAD_DIM))
        new_v_p.append(vp.reshape(batch, seq, N_HEADS, HEAD_DIM))
        new_k_s.append(ks.reshape(n_seq, dec_seq, N_HEADS, HEAD_DIM))
        new_v_s.append(vs.reshape(n_seq, dec_seq, N_HEADS, HEAD_DIM))

    return (xp.reshape(batch, seq, D_MODEL), xs.reshape(n_seq, dec_seq, D_MODEL),
            jnp.stack(new_k_p), jnp.stack(new_v_p), jnp.stack(new_ki_p),
            jnp.stack(new_k_s), jnp.stack(new_v_s), jnp.stack(new_ki_s))
```

```python
import functools
import math

import jax
import jax.numpy as jnp
from jax import lax
from jax.experimental import pallas as pl
from jax.experimental.pallas import tpu as pltpu

F32 = jnp.float32
BF16 = jnp.bfloat16
I32 = jnp.int32

D_MODEL = 1024
N_HEADS = 16
HEAD_DIM = 64
HHD = N_HEADS * HEAD_DIM
LANES = 128
N_PAIRS = HHD // LANES
D_FF = 2816
FF_CHUNK = 256
N_IDX_HEADS = 16
D_IDX = 64
DSA_TOPK = 256
MOBA_BLOCK = 256
MOBA_TOPK = 3
N_BUCKETS = 32
MAX_EXACT = 16
MAX_DISTANCE = 128
PAGE_SIZE = 128
QB = 256
CH = 256
EPS = 1e-6
NEG = -1e30
INT_MIN = -(2 ** 31)
EXP_ZERO_BELOW = -104.0
SCALE = HEAD_DIM ** -0.5
IDX_SCALE = (N_IDX_HEADS * D_IDX) ** -0.5
VMEM_LIMIT = 56 * 1024 * 1024
TOKEN_TILE = 512

KIND_DSA, KIND_MOBA, KIND_SB = 0, 1, 2


def _cparams(*semantics):
    return pltpu.CompilerParams(dimension_semantics=semantics, vmem_limit_bytes=VMEM_LIMIT)


def _dot(a, b):
    return jnp.dot(a, b, preferred_element_type=F32)


def _dot_t(a, b):
    return lax.dot_general(a, b, (((1,), (1,)), ((), ())), preferred_element_type=F32)


def _split_hi_lo(v):
    hi = v.astype(BF16)
    lo = (v - hi.astype(F32)).astype(BF16)
    return hi, lo


def _rms(x, g):
    ms = jnp.mean(x * x, axis=-1, keepdims=True)
    return x * lax.rsqrt(ms + EPS) * g


def _softplus(z):
    return jnp.maximum(z, 0.0) + jnp.log(1.0 + jnp.exp(-jnp.abs(z)))


def _ffn_body(x_ref, g_ref, wgu_ref, wd_ref, o_ref):
    x = x_ref[...]
    xn = _rms(x, g_ref[...]).astype(BF16)
    acc = jnp.zeros_like(x)
    for c in range(D_FF // FF_CHUNK):
        lo = c * FF_CHUNK
        a = _dot(xn, wgu_ref[:, lo:lo + FF_CHUNK])
        b = _dot(xn, wgu_ref[:, D_FF + lo:D_FF + lo + FF_CHUNK])
        h = (a * (1.0 / (1.0 + jnp.exp(-a))) * b).astype(BF16)
        acc = acc + _dot(h, wd_ref[lo:lo + FF_CHUNK, :])
    o_ref[...] = x + 0.5 * acc


def _ffn(x, g, wgu, wd):
    n = x.shape[0]
    tm = min(TOKEN_TILE, n)
    return pl.pallas_call(
        _ffn_body,
        grid=(n // tm,),
        in_specs=[pl.BlockSpec((tm, D_MODEL), lambda i: (i, 0)),
                  pl.BlockSpec((1, D_MODEL), lambda i: (0, 0)),
                  pl.BlockSpec((D_MODEL, 2 * D_FF), lambda i: (0, 0)),
                  pl.BlockSpec((D_FF, D_MODEL), lambda i: (0, 0))],
        out_specs=pl.BlockSpec((tm, D_MODEL), lambda i: (i, 0)),
        out_shape=jax.ShapeDtypeStruct(x.shape, F32),
        compiler_params=_cparams("parallel"),
        name="ffn",
    )(x, g.reshape(1, D_MODEL), wgu, wd)


def _head_rms(y, g_row):
    r_i = lax.broadcasted_iota(I32, (HHD, LANES), 0) >> 6
    c_i = lax.broadcasted_iota(I32, (HHD, LANES), 1)
    gather = jnp.where(r_i == c_i, 1.0, 0.0).astype(BF16)
    r2 = lax.broadcasted_iota(I32, (LANES, HHD), 0)
    c2 = lax.broadcasted_iota(I32, (LANES, HHD), 1) >> 6
    spread = jnp.where(r2 == c2, 1.0, 0.0).astype(BF16)
    hi, lo = _split_hi_lo(y * y)
    ssq = _dot(hi, gather) + _dot(lo, gather)
    r = lax.rsqrt(ssq * (1.0 / HEAD_DIM) + EPS)
    rhi, rlo = _split_hi_lo(r)
    rfull = _dot(rhi, spread) + _dot(rlo, spread)
    return y * rfull * g_row


def _store_pairs(ref, y):
    for p in range(N_PAIRS):
        ref[p] = y[:, p * LANES:(p + 1) * LANES].astype(BF16)


def _proj_body(kind, with_kmean, *refs):
    if kind == KIND_DSA:
        (x_ref, g_ref, w_ref, qg_ref, kg_ref, wt_ref, kig_ref,
         k_out, v_out, q_pm, k_pm, v_pm, vt_pm, qi_pm, ki_out, ki2_out, wi_out) = refs
    elif kind == KIND_MOBA:
        x_ref, g_ref, w_ref, qg_ref, kg_ref, k_out, v_out, q_pm, k_pm, v_pm, vt_pm = refs[:11]
        kmean_out = refs[11] if with_kmean else None
    else:
        x_ref, g_ref, w_ref, k_out, v_out, q_pm, k_pm, v_pm, vt_pm = refs
    xn = _rms(x_ref[...], g_ref[...]).astype(BF16)
    q = _dot(xn, w_ref[:, 0:HHD])
    k = _dot(xn, w_ref[:, HHD:2 * HHD])
    v = _dot(xn, w_ref[:, 2 * HHD:3 * HHD])
    if kind != KIND_SB:
        q = _head_rms(q, qg_ref[...])
        k = _head_rms(k, kg_ref[...])
    k_out[...] = k
    v_out[...] = v
    _store_pairs(q_pm, q * SCALE)
    _store_pairs(k_pm, k)
    _store_pairs(v_pm, v)
    for p in range(N_PAIRS):
        vt_pm[p] = v[:, p * LANES:(p + 1) * LANES].T.astype(BF16)
    if kind == KIND_DSA:
        _store_pairs(qi_pm, _dot(xn, w_ref[:, 3 * HHD:4 * HHD]))
        ki2 = _dot(xn, wt_ref[:, 0:LANES])
        ki2 = ki2 * lax.rsqrt(jnp.mean(ki2 * ki2, axis=-1, keepdims=True) + EPS) * kig_ref[...]
        ki_out[...] = ki2[:, 0:D_IDX]
        ki2_out[...] = ki2.astype(BF16)
        wi_out[...] = _dot(xn, wt_ref[:, LANES:2 * LANES])[:, 0:N_IDX_HEADS] * IDX_SCALE
    if kind == KIND_MOBA and with_kmean:
        tm = k.shape[0]
        for j in range(tm // MOBA_BLOCK):
            kmean_out[j] = jnp.mean(k[j * MOBA_BLOCK:(j + 1) * MOBA_BLOCK], axis=0, keepdims=True)


def _proj(kind, x, g, w, qg=None, kg=None, w_tail=None, kig=None, with_kmean=False):
    n = x.shape[0]
    tm = min(TOKEN_TILE, n)
    row = lambda i: (i, 0)
    fixed = lambda i: (0, 0)
    pm_spec = pl.BlockSpec((N_PAIRS, tm, LANES), lambda i: (0, i, 0))
    pm_shape = jax.ShapeDtypeStruct((N_PAIRS, n, LANES), BF16)
    vt_spec = pl.BlockSpec((N_PAIRS, LANES, tm), lambda i: (0, 0, i))
    vt_shape = jax.ShapeDtypeStruct((N_PAIRS, LANES, n), BF16)
    args = [x, g.reshape(1, D_MODEL), w]
    in_specs = [pl.BlockSpec((tm, D_MODEL), row), pl.BlockSpec((1, D_MODEL), fixed),
                pl.BlockSpec(w.shape, fixed)]
    if kind != KIND_SB:
        args += [jnp.tile(qg, N_HEADS).reshape(1, HHD), jnp.tile(kg, N_HEADS).reshape(1, HHD)]
        in_specs += [pl.BlockSpec((1, HHD), fixed)] * 2
    if kind == KIND_DSA:
        args += [w_tail, jnp.tile(kig, LANES // D_IDX).reshape(1, LANES)]
        in_specs += [pl.BlockSpec(w_tail.shape, fixed), pl.BlockSpec((1, LANES), fixed)]
    out_shape = [jax.ShapeDtypeStruct((n, HHD), F32)] * 2 + [pm_shape] * 3 + [vt_shape]
    out_specs = [pl.BlockSpec((tm, HHD), row)] * 2 + [pm_spec] * 3 + [vt_spec]
    if kind == KIND_DSA:
        out_shape += [pm_shape, jax.ShapeDtypeStruct((n, D_IDX), F32),
                      jax.ShapeDtypeStruct((n, LANES), BF16),
                      jax.ShapeDtypeStruct((n, N_IDX_HEADS), F32)]
        out_specs += [pm_spec, pl.BlockSpec((tm, D_IDX), row), pl.BlockSpec((tm, LANES), row),
                      pl.BlockSpec((tm, N_IDX_HEADS), row)]
    if kind == KIND_MOBA and with_kmean:
        out_shape += [jax.ShapeDtypeStruct((n // MOBA_BLOCK, 1, HHD), F32)]
        out_specs += [pl.BlockSpec((tm // MOBA_BLOCK, 1, HHD), lambda i: (i, 0, 0))]
    return pl.pallas_call(
        functools.partial(_proj_body, kind, with_kmean),
        grid=(n // tm,), in_specs=in_specs, out_specs=out_specs, out_shape=out_shape,
        compiler_params=_cparams("parallel"), name="proj",
    )(*args)


def _outproj_body(x_ref, o_ref, w_ref, y_ref):
    o = jnp.concatenate([o_ref[p] for p in range(N_PAIRS)], axis=1)
    y_ref[...] = x_ref[...] + _dot(o, w_ref[...])


def _outproj(x, o_pm, w):
    n = x.shape[0]
    tm = min(TOKEN_TILE, n)
    return pl.pallas_call(
        _outproj_body,
        grid=(n // tm,),
        in_specs=[pl.BlockSpec((tm, D_MODEL), lambda i: (i, 0)),
                  pl.BlockSpec((N_PAIRS, tm, LANES), lambda i: (0, i, 0)),
                  pl.BlockSpec((HHD, D_MODEL), lambda i: (0, 0))],
        out_specs=pl.BlockSpec((tm, D_MODEL), lambda i: (i, 0)),
        out_shape=jax.ShapeDtypeStruct(x.shape, F32),
        compiler_params=_cparams("parallel"), name="outproj",
    )(x, o_pm, w)


def _sortable_key(score):
    bits = pltpu.bitcast(score + 0.0, I32)
    return bits ^ ((bits >> 31) & 0x7FFFFFFF)


def _topk_threshold(count_ge, shape, topk):
    zero = jnp.zeros(shape, I32)
    t0 = jnp.where(count_ge(zero) >= topk, zero, jnp.full(shape, INT_MIN, I32))

    def bit_step(i, t):
        cand = t + jnp.left_shift(jnp.int32(1), jnp.int32(30) - i)
        return jnp.where(count_ge(cand) >= topk, cand, t)

    return lax.fori_loop(0, 31, bit_step, t0)


def _top3_flags(gate, axis):
    idx = lax.broadcasted_iota(I32, gate.shape, axis)
    sel = jnp.zeros(gate.shape, F32)
    for _ in range(MOBA_TOPK):
        mx = jnp.max(gate, axis=axis, keepdims=True)
        cand = jnp.where((gate == mx) & (gate > -jnp.inf), idx, gate.shape[axis])
        first = jnp.min(cand, axis=axis, keepdims=True)
        pick = idx == first
        sel = jnp.where(pick, 1.0, sel)
        gate = jnp.where(pick, -jnp.inf, gate)
    return sel


def _stack_masked_q(q2):
    lane = lax.broadcasted_iota(I32, q2.shape, 1)
    zero = jnp.zeros_like(q2)
    return jnp.concatenate([jnp.where(lane < HEAD_DIM, q2, zero),
                            jnp.where(lane >= HEAD_DIM, q2, zero)], axis=0)


def _pair_out(acc_p, scale_p=None):
    a, b = acc_p[0:HEAD_DIM, 0:QB], acc_p[HEAD_DIM:LANES, QB:2 * QB]
    if scale_p is not None:
        a, b = a * scale_p[:, 0:QB], b * scale_p[:, QB:2 * QB]
    return jnp.concatenate([a, b], axis=0).T.astype(BF16)


def _flash_step_t(s_t, m, l, acc, v_t):
    m_new = jnp.maximum(m, jnp.max(s_t, axis=0, keepdims=True))
    alpha = jnp.exp(m - m_new)
    p = jnp.exp(s_t - m_new)
    l = alpha * l + jnp.sum(p, axis=0, keepdims=True)
    acc = alpha * acc + _dot(v_t, p.astype(BF16))
    return m_new, l, acc


def _dsa_mask_body(topk, nq, qi_ref, ki2_ref, wit_ref, o_ref, key_ref, qim_ref):
    qb = pl.program_id(1)
    nck = qb + 1
    for p in range(N_PAIRS):
        qim_ref[p] = _stack_masked_q(qi_ref[p])
    wt = wit_ref[...]
    rowk = lax.broadcasted_iota(I32, (CH, QB), 0)
    laneq = lax.broadcasted_iota(I32, (CH, QB), 1)

    def score_chunk(c, carry):
        ki = ki2_ref[pl.ds(pl.multiple_of(c * CH, CH), CH), :]
        s = jnp.zeros((CH, QB), F32)
        for p in range(N_PAIRS):
            d = jnp.maximum(_dot_t(ki, qim_ref[p]), 0.0)
            s = s + wt[2 * p:2 * p + 1] * d[:, 0:QB] + wt[2 * p + 1:2 * p + 2] * d[:, QB:2 * QB]
        s = jnp.where(c * CH + rowk <= qb * QB + laneq, s, -jnp.inf)
        key_ref[c] = _sortable_key(s)
        return carry

    lax.fori_loop(0, nck, score_chunk, 0)

    def count_ge(cand):
        def body(c, acc):
            hit = jnp.where(key_ref[c] >= cand, 1.0, 0.0)
            return acc + jnp.sum(hit.reshape(CH // 8, 8, QB), axis=0)

        acc = lax.fori_loop(0, nck, body, jnp.zeros((8, QB), F32))
        return jnp.sum(acc, axis=0, keepdims=True)

    thr = _topk_threshold(count_ge, (1, QB), float(topk))

    def mask_chunk(c, carry):
        o_ref[pl.ds(pl.multiple_of(c * CH, CH), CH), :] = jnp.where(key_ref[c] >= thr, 0.0, NEG).astype(BF16)
        return carry

    lax.fori_loop(0, nck, mask_chunk, 0)

    def fill_chunk(c, carry):
        o_ref[pl.ds(pl.multiple_of(c * CH, CH), CH), :] = jnp.full((CH, QB), NEG, BF16)
        return carry

    lax.fori_loop(nck, nq, fill_chunk, 0)


def _dsa_mask(batch, seq, qi_pm, ki2, wi_t):
    nq = seq // QB
    topk = min(DSA_TOPK, seq // 4)
    return pl.pallas_call(
        functools.partial(_dsa_mask_body, topk, nq),
        grid=(batch, nq),
        in_specs=[pl.BlockSpec((N_PAIRS, QB, LANES), lambda b, i: (0, b * nq + i, 0)),
                  pl.BlockSpec((seq, LANES), lambda b, i: (b, 0)),
                  pl.BlockSpec((N_IDX_HEADS, QB), lambda b, i: (0, b * nq + i))],
        out_specs=pl.BlockSpec((None, seq, QB), lambda b, i: (b, 0, i)),
        out_shape=jax.ShapeDtypeStruct((batch, seq, seq), BF16),
        scratch_shapes=[pltpu.VMEM((nq, CH, QB), I32), pltpu.VMEM((N_PAIRS, 2 * QB, LANES), BF16)],
        compiler_params=_cparams("parallel", "parallel"), name="dsa_mask",
    )(qi_pm, ki2, wi_t)


def _masked_flash_body(kind, q_ref, k_ref, vt_ref, bias_ref, x_ref, o_ref,
                       qmm_ref, m_ref, l_ref, acc_ref, *sel):
    qb = pl.program_id(1)
    c = pl.program_id(2)

    @pl.when(c == 0)
    def _init():
        for p in range(N_PAIRS):
            qmm = _stack_masked_q(q_ref[p])
            qmm_ref[p] = qmm
            m_ref[p] = jnp.full((1, 2 * QB), NEG, F32)
            l_ref[p] = jnp.zeros((1, 2 * QB), F32)
            acc_ref[p] = jnp.zeros((LANES, 2 * QB), F32)
            if kind == KIND_MOBA:
                sel_ref = sel[0]
                n_rows = sel_ref.shape[1]
                gate = _dot_t(x_ref[p], qmm)[0:n_rows]
                blk = lax.broadcasted_iota(I32, gate.shape, 0)
                sel_ref[p] = _top3_flags(jnp.where(blk < qb, gate, -jnp.inf), 0)

    @pl.when(c <= qb)
    def _step():
        if kind == KIND_DSA:
            mask = x_ref[...].astype(F32)
            mask2 = jnp.concatenate([mask, mask], axis=1)
        for p in range(N_PAIRS):
            s_t = _dot_t(k_ref[p], qmm_ref[p]) + bias_ref[p]
            if kind == KIND_DSA:
                s_t = s_t + mask2
            else:
                vis = sel[0][p, pl.ds(c, 1), :] + jnp.where(c == qb, 1.0, 0.0)
                s_t = s_t + jnp.where(vis > 0.5, 0.0, NEG)
            m, l, acc = _flash_step_t(s_t, m_ref[p], l_ref[p], acc_ref[p], vt_ref[p])
            m_ref[p] = m
            l_ref[p] = l
            acc_ref[p] = acc

    @pl.when(c == qb)
    def _finish():
        for p in range(N_PAIRS):
            o_ref[p] = _pair_out(acc_ref[p], 1.0 / l_ref[p])


def _masked_flash(kind, batch, seq, q_pm, k_pm, vt_pm, bias, extra):
    nq = seq // QB
    q_spec = pl.BlockSpec((N_PAIRS, QB, LANES), lambda b, i, c: (0, b * nq + i, 0))
    in_specs = [q_spec,
                pl.BlockSpec((N_PAIRS, CH, LANES), lambda b, i, c: (0, b * nq + jnp.minimum(c, i), 0)),
                pl.BlockSpec((N_PAIRS, LANES, CH), lambda b, i, c: (0, 0, b * nq + jnp.minimum(c, i))),
                pl.BlockSpec((N_PAIRS, None, CH, 2 * QB),
                             lambda b, i, c: (0, jnp.clip(i - c, 0, 2), 0, 0))]
    scratch = [pltpu.VMEM((N_PAIRS, 2 * QB, LANES), BF16), pltpu.VMEM((N_PAIRS, 1, 2 * QB), F32),
               pltpu.VMEM((N_PAIRS, 1, 2 * QB), F32), pltpu.VMEM((N_PAIRS, LANES, 2 * QB), F32)]
    if kind == KIND_DSA:
        in_specs.append(pl.BlockSpec((None, CH, QB), lambda b, i, c: (b, jnp.minimum(c, i), i)))
    else:
        in_specs.append(pl.BlockSpec((N_PAIRS, None, LANES, LANES), lambda b, i, c: (0, b, 0, 0)))
        scratch.append(pltpu.VMEM((N_PAIRS, -(-nq // 8) * 8, 2 * QB), F32))
    return pl.pallas_call(
        functools.partial(_masked_flash_body, kind),
        grid=(batch, nq, nq), in_specs=in_specs, out_specs=q_spec,
        out_shape=jax.ShapeDtypeStruct((N_PAIRS, batch * seq, LANES), BF16),
        scratch_shapes=scratch,
        compiler_params=_cparams("parallel", "parallel", "arbitrary"), name="masked_flash",
    )(q_pm, k_pm, vt_pm, bias, extra)


def _sb_flash_body(q_ref, k_ref, vt_ref, o_ref, qmm_ref, tail_ref, acc_ref, live_ref):
    qb = pl.program_id(1)
    c = pl.program_id(2)

    @pl.when(c == 0)
    def _init():
        for p in range(N_PAIRS):
            qmm_ref[p] = _stack_masked_q(q_ref[p])
            tail_ref[p] = jnp.zeros((1, 2 * QB), F32)
            acc_ref[p] = jnp.zeros((LANES, 2 * QB), F32)
        live_ref[0] = 1

    @pl.when((c <= qb) & (live_ref[0] == 1))
    def _step():
        rowk = lax.broadcasted_iota(I32, (CH, 2 * QB), 0)
        laneq = lax.broadcasted_iota(I32, (CH, 2 * QB), 1) & (QB - 1)
        strict = (qb - c) * CH + rowk < qb * QB + laneq
        later = (lax.broadcasted_iota(I32, (CH, CH), 1) > lax.broadcasted_iota(I32, (CH, CH), 0))
        later = jnp.where(later, 1.0, 0.0).astype(BF16)
        worst = jnp.full((1, 1), -jnp.inf, F32)
        for p in range(N_PAIRS):
            z = _dot_t(k_ref[p], qmm_ref[p])
            sp = _softplus(z)
            lr = jnp.where(strict, -sp, 0.0)
            hi, lo = _split_hi_lo(lr)
            tail = tail_ref[p]
            a = jnp.where(strict, jnp.exp(z - sp + tail + _dot(later, hi) + _dot(later, lo)), 0.0)
            acc_ref[p] = acc_ref[p] + _dot(vt_ref[p], a.astype(BF16))
            tail = tail + jnp.sum(lr, axis=0, keepdims=True)
            tail_ref[p] = tail
            worst = jnp.maximum(worst, jnp.max(tail, axis=1, keepdims=True))
        live_ref[0] = jnp.where(worst[0, 0] > EXP_ZERO_BELOW, 1, 0)

    @pl.when(c == qb)
    def _finish():
        for p in range(N_PAIRS):
            o_ref[p] = _pair_out(acc_ref[p])


def _sb_flash(batch, seq, q_pm, k_pm, vt_pm):
    nq = seq // QB
    q_spec = pl.BlockSpec((N_PAIRS, QB, LANES), lambda b, i, c: (0, b * nq + i, 0))
    return pl.pallas_call(
        _sb_flash_body,
        grid=(batch, nq, nq),
        in_specs=[q_spec,
                  pl.BlockSpec((N_PAIRS, CH, LANES), lambda b, i, c: (0, b * nq + jnp.maximum(i - c, 0), 0)),
                  pl.BlockSpec((N_PAIRS, LANES, CH), lambda b, i, c: (0, 0, b * nq + jnp.maximum(i - c, 0)))],
        out_specs=q_spec,
        out_shape=jax.ShapeDtypeStruct((N_PAIRS, batch * seq, LANES), BF16),
        scratch_shapes=[pltpu.VMEM((N_PAIRS, 2 * QB, LANES), BF16), pltpu.VMEM((N_PAIRS, 1, 2 * QB), F32),
                        pltpu.VMEM((N_PAIRS, LANES, 2 * QB), F32), pltpu.SMEM((1,), I32)],
        compiler_params=_cparams("parallel", "parallel", "arbitrary"), name="sb_flash",
    )(q_pm, k_pm, vt_pm)


N_ROWS = 64
PAGES_PER_STEP = 8
PAGES_PER_BLOCK = MOBA_BLOCK // PAGE_SIZE


def _softmax_update(s, v_t, m_ref, l_ref, acc_ref):
    m_old = m_ref[...]
    m_new = jnp.maximum(m_old, jnp.max(s, axis=1, keepdims=True))
    alpha = jnp.exp(m_old - m_new)
    p = jnp.exp(s - m_new)
    m_ref[...] = m_new
    l_ref[...] = alpha * l_ref[...] + jnp.sum(p, axis=1, keepdims=True)
    acc_ref[...] = alpha * acc_ref[...] + _dot_t(p.astype(BF16), v_t)


def _dec_init(m_ref, l_ref, acc_ref):
    m_ref[...] = jnp.full((N_ROWS, 1), NEG, F32)
    l_ref[...] = jnp.zeros((N_ROWS, 1), F32)
    acc_ref[...] = jnp.zeros(acc_ref.shape, F32)


def _rows_from_tokens(x8, dec_seq):
    return jnp.concatenate([jnp.broadcast_to(x8[t:t + 1], (N_HEADS, x8.shape[1]))
                            for t in range(dec_seq)], axis=0)


def _lane_concat_bf16(page_refs):
    return jnp.concatenate([ref[...].astype(BF16) for ref in page_refs], axis=1)


def _page_bias(bias_ref, pg, n_pages):
    return bias_ref[jnp.where(pg == n_pages - 1, 1, 0)]


def _dsa_sample_body(n_pages, dec_seq, topk, pt_ref, q_ref, qi_ref, wib_ref, *refs):
    k_pages = refs[0:PAGES_PER_STEP]
    v_pages = refs[PAGES_PER_STEP:2 * PAGES_PER_STEP]
    ki_pages = refs[2 * PAGES_PER_STEP:2 * PAGES_PER_STEP + n_pages]
    (knew_ref, vnew_ref, kinew_ref, bias_ref, o_ref,
     mb_ref, m_ref, l_ref, acc_ref) = refs[2 * PAGES_PER_STEP + n_pages:]
    j = pl.program_id(1)
    last = n_pages // PAGES_PER_STEP - 1
    q = q_ref[...]

    @pl.when(j == 0)
    def _select():
        qi = qi_ref[...]
        wib = wib_ref[...]
        rows8 = lax.broadcasted_iota(I32, (8, PAGE_SIZE), 0)
        lane8 = lax.broadcasted_iota(I32, (8, PAGE_SIZE), 1)
        keys = []
        for pg in range(n_pages + 1):
            ki_t = ki_pages[pg][...].astype(BF16) if pg < n_pages else kinew_ref[...]
            wr = wib * jnp.maximum(_dot(qi, ki_t), 0.0)
            per_t = [jnp.sum(wr[t * N_HEADS:(t + 1) * N_HEADS], axis=0, keepdims=True)
                     for t in range(dec_seq)]
            sc = jnp.concatenate(per_t + [jnp.zeros((8 - dec_seq, PAGE_SIZE), F32)], axis=0)
            valid = rows8 < dec_seq
            if pg == n_pages:
                valid = valid & (lane8 <= rows8)
            keys.append(_sortable_key(jnp.where(valid, sc, -jnp.inf)))

        def count_ge(cand):
            acc = jnp.zeros((8, PAGE_SIZE), F32)
            for kk in keys:
                acc = acc + jnp.where(kk >= cand, 1.0, 0.0)
            return jnp.sum(acc, axis=1, keepdims=True)

        thr = _topk_threshold(count_ge, (8, 1), float(topk))
        for pg in range(n_pages + 1):
            mb_ref[pg] = jnp.where(keys[pg] >= thr, 0.0, NEG)
        _dec_init(m_ref, l_ref, acc_ref)

    first = PAGES_PER_STEP * j
    add = jnp.concatenate([_page_bias(bias_ref, first + r, n_pages)
                           + _rows_from_tokens(mb_ref[first + r], dec_seq)
                           for r in range(PAGES_PER_STEP)], axis=1)
    s = _dot(q, _lane_concat_bf16(k_pages)) + add
    _softmax_update(s, _lane_concat_bf16(v_pages), m_ref, l_ref, acc_ref)

    @pl.when(j == last)
    def _finish():
        s = _dot(q, knew_ref[...]) + bias_ref[2] + _rows_from_tokens(mb_ref[n_pages], dec_seq)
        _softmax_update(s, vnew_ref[...], m_ref, l_ref, acc_ref)
        o_ref[...] = acc_ref[...] / l_ref[...]


def _moba_sample_body(n_pages, pt_ref, q_ref, *refs):
    k_pages = refs[0:PAGES_PER_STEP]
    v_pages = refs[PAGES_PER_STEP:2 * PAGES_PER_STEP]
    (knew_ref, vnew_ref, bias_ref, o_ref,
     gate_ref, pm_ref, pl_ref, pacc_ref, m_ref, l_ref, acc_ref) = refs[2 * PAGES_PER_STEP:]
    j = pl.program_id(1)
    n_blk = n_pages // PAGES_PER_BLOCK
    last = n_pages // PAGES_PER_STEP - 1
    q = q_ref[...]
    lane = lax.broadcasted_iota(I32, (N_ROWS, LANES), 1)

    @pl.when(j == 0)
    def _zero():
        gate_ref[...] = jnp.zeros((N_ROWS, LANES), F32)

    for bi in range(PAGES_PER_STEP // PAGES_PER_BLOCK):
        blk = (PAGES_PER_STEP // PAGES_PER_BLOCK) * j + bi
        _dec_init(m_ref, l_ref, acc_ref)
        rs = range(bi * PAGES_PER_BLOCK, (bi + 1) * PAGES_PER_BLOCK)
        raw = _dot(q, _lane_concat_bf16([k_pages[r] for r in rs]))
        logit_sum = jnp.sum(raw, axis=1, keepdims=True)
        add = jnp.concatenate([_page_bias(bias_ref, PAGES_PER_STEP * j + r, n_pages) for r in rs], axis=1)
        _softmax_update(raw + add, _lane_concat_bf16([v_pages[r] for r in rs]), m_ref, l_ref, acc_ref)
        gate_ref[...] = gate_ref[...] + jnp.where(lane == blk, logit_sum * (1.0 / MOBA_BLOCK), 0.0)
        pm_ref[blk] = m_ref[...]
        pl_ref[blk] = l_ref[...]
        pacc_ref[blk] = acc_ref[...]

    @pl.when(j == last)
    def _finish():
        sel = _top3_flags(jnp.where(lane < n_blk, gate_ref[...], -jnp.inf), 1)
        _dec_init(m_ref, l_ref, acc_ref)
        _softmax_update(_dot(q, knew_ref[...]) + bias_ref[2], vnew_ref[...], m_ref, l_ref, acc_ref)
        picked = [sel[:, n:n + 1] > 0.5 for n in range(n_blk)]
        m_all = m_ref[...]
        for n in range(n_blk):
            m_all = jnp.maximum(m_all, jnp.where(picked[n], pm_ref[n], NEG))
        w_own = jnp.exp(m_ref[...] - m_all)
        l_all = w_own * l_ref[...]
        acc = w_own * acc_ref[...]
        for n in range(n_blk):
            w = jnp.where(picked[n], jnp.exp(pm_ref[n] - m_all), 0.0)
            l_all = l_all + w * pl_ref[n]
            acc = acc + w * pacc_ref[n]
        o_ref[...] = acc / l_all


def _sb_sample_body(n_pages, dec_seq, pt_ref, q_ref, *refs):
    k_pages = refs[0:PAGES_PER_STEP]
    v_pages = refs[PAGES_PER_STEP:2 * PAGES_PER_STEP]
    knew_ref, vnew_ref, o_ref, tail_ref, acc_ref, live_ref = refs[2 * PAGES_PER_STEP:]
    j = pl.program_id(1)
    q = q_ref[...]
    n = PAGE_SIZE

    def chunk(z, strict, v_t):
        jj = lax.broadcasted_iota(I32, (n, 2 * n), 0)
        ss = lax.broadcasted_iota(I32, (n, 2 * n), 1)
        ue = jnp.where((jj > ss) | (ss >= n), 1.0, 0.0).astype(BF16)
        sp = _softplus(z)
        lr = -sp if strict is None else jnp.where(strict, -sp, 0.0)
        hi, lo = _split_hi_lo(lr)
        cs = _dot(hi, ue) + _dot(lo, ue)
        a = jnp.exp(z - sp + tail_ref[...] + cs[:, :n])
        if strict is not None:
            a = jnp.where(strict, a, 0.0)
        acc_ref[...] = acc_ref[...] + _dot_t(a.astype(BF16), v_t)
        tail_ref[...] = tail_ref[...] + cs[:, n:]

    @pl.when(j == 0)
    def _newest():
        tail_ref[...] = jnp.zeros((N_ROWS, n), F32)
        acc_ref[...] = jnp.zeros((N_ROWS, HHD), F32)
        live_ref[0] = 1
        tok = lax.broadcasted_iota(I32, (N_ROWS, n), 0) // N_HEADS
        col = lax.broadcasted_iota(I32, (N_ROWS, n), 1)
        chunk(_dot(q, knew_ref[...]), col < tok, vnew_ref[...])

    @pl.when(live_ref[0] == 1)
    def _pages():
        for r in reversed(range(PAGES_PER_STEP)):
            chunk(_dot(q, k_pages[r][...].astype(BF16)), None, v_pages[r][...].astype(BF16))
        worst = jnp.max(jnp.max(tail_ref[...], axis=1, keepdims=True), axis=0, keepdims=True)
        live_ref[0] = jnp.where(worst[0, 0] > EXP_ZERO_BELOW, 1, 0)

    @pl.when(j == n_pages // PAGES_PER_STEP - 1)
    def _finish():
        o_ref[...] = acc_ref[...]


def _sample_attn(kind, layer, page_table, cache_kt, cache_vt, qbd, knew_t, vnew_t, bias=None, dsa=None):
    n_seq, n_pages = page_table.shape
    n_steps = n_pages // PAGES_PER_STEP
    dec_seq = N_ROWS // N_HEADS

    def seq_spec(shape):
        return pl.BlockSpec((None,) + tuple(shape), lambda b, j, pt: (b,) + (0,) * len(shape))

    def page_spec(lyr, n_rows, page_of):
        return pl.BlockSpec((None, None, n_rows, PAGE_SIZE),
                            lambda b, j, pt: (lyr, pt[b, page_of(j)], 0, 0))

    if kind == KIND_SB:
        first = lambda j: n_pages - PAGES_PER_STEP * (j + 1)
    else:
        first = lambda j: PAGES_PER_STEP * j
    group = [page_spec(layer, HHD, functools.partial(lambda j, r: first(j) + r, r=r))
             for r in range(PAGES_PER_STEP)]
    args = [qbd]
    in_specs = [seq_spec((N_ROWS, HHD))]
    if kind == KIND_DSA:
        qi_r, wib, cache_kit, slot, kinew_t = dsa
        args += [qi_r, wib]
        in_specs += [seq_spec((N_ROWS, D_IDX)), seq_spec((N_ROWS, PAGE_SIZE))]
    args += [cache_kt] * PAGES_PER_STEP + [cache_vt] * PAGES_PER_STEP
    in_specs += group * 2
    if kind == KIND_DSA:
        args += [cache_kit] * n_pages
        in_specs += [page_spec(slot, D_IDX, functools.partial(lambda j, pg: pg, pg=pg))
                     for pg in range(n_pages)]
    args += [knew_t, vnew_t]
    in_specs += [seq_spec((HHD, PAGE_SIZE))] * 2
    if kind == KIND_DSA:
        args += [kinew_t]
        in_specs += [seq_spec((D_IDX, PAGE_SIZE))]
    if kind != KIND_SB:
        args += [bias]
        in_specs += [pl.BlockSpec((3, N_ROWS, PAGE_SIZE), lambda b, j, pt: (0, 0, 0))]
    acc_shape = pltpu.VMEM((N_ROWS, HHD), F32)
    stat_shape = pltpu.VMEM((N_ROWS, 1), F32)
    if kind == KIND_DSA:
        topk = min(DSA_TOPK, (n_pages * PAGE_SIZE + dec_seq) // 4)
        body = functools.partial(_dsa_sample_body, n_pages, dec_seq, topk)
        scratch = [pltpu.VMEM((n_pages + 1, 8, PAGE_SIZE), F32), stat_shape, stat_shape, acc_shape]
    elif kind == KIND_MOBA:
        n_blk = n_pages // PAGES_PER_BLOCK
        body = functools.partial(_moba_sample_body, n_pages)
        scratch = [pltpu.VMEM((N_ROWS, LANES), F32),
                   pltpu.VMEM((n_blk, N_ROWS, 1), F32), pltpu.VMEM((n_blk, N_ROWS, 1), F32),
                   pltpu.VMEM((n_blk, N_ROWS, HHD), F32), stat_shape, stat_shape, acc_shape]
    else:
        body = functools.partial(_sb_sample_body, n_pages, dec_seq)
        scratch = [pltpu.VMEM((N_ROWS, PAGE_SIZE), F32), acc_shape, pltpu.SMEM((1,), I32)]
    return pl.pallas_call(
        body,
        grid_spec=pltpu.PrefetchScalarGridSpec(
            num_scalar_prefetch=1, grid=(n_seq, n_steps), in_specs=in_specs,
            out_specs=seq_spec((N_ROWS, HHD)), scratch_shapes=scratch),
        out_shape=jax.ShapeDtypeStruct((n_seq, N_ROWS, HHD), F32),
        compiler_params=_cparams("parallel", "arbitrary"), name="sample_attn",
    )(page_table, *args)


def _rel_bucket(dist):
    n = jnp.maximum(dist, 0)
    nf = jnp.maximum(n, 1).astype(F32)
    large = MAX_EXACT + (jnp.log(nf / MAX_EXACT) / math.log(MAX_DISTANCE / MAX_EXACT)
                         * (N_BUCKETS - MAX_EXACT)).astype(I32)
    large = jnp.minimum(large, N_BUCKETS - 1)
    return jnp.where(n < MAX_EXACT, n, large)


def _bias_lookup(rel_table, dist, shift=None):
    b = jnp.moveaxis(rel_table[_rel_bucket(dist)], -1, 0)
    if shift is not None:
        b = b - shift.reshape((N_HEADS,) + (1,) * dist.ndim)
    return jnp.where(dist[None] >= 0, b, NEG)


def _prompt_bias_tiles(rel_table):
    far = rel_table[_rel_bucket(jnp.int32(2 * QB))]
    j = jnp.arange(CH, dtype=I32)[:, None]
    i = jnp.arange(QB, dtype=I32)[None, :]
    near = [_bias_lookup(rel_table, u * CH + i - j, far) for u in range(2)]
    tiles = jnp.stack(near + [jnp.zeros_like(near[0])], axis=1)
    tiles = tiles.reshape(N_PAIRS, 2, 3, CH, QB)
    return jnp.concatenate([tiles[:, 0], tiles[:, 1]], axis=-1).astype(F32)


def _sample_bias_tiles(rel_table, dec_seq):
    t = jnp.arange(dec_seq, dtype=I32)[:, None]
    j = jnp.arange(PAGE_SIZE, dtype=I32)[None, :]
    far = _bias_lookup(rel_table, jnp.full((dec_seq, PAGE_SIZE), 2 * PAGE_SIZE, I32))
    last = _bias_lookup(rel_table, PAGE_SIZE + t - j)
    new = _bias_lookup(rel_table, jnp.where(j < dec_seq, t - j, -1))
    tiles = jnp.stack([far, last, new], axis=0)
    return jnp.transpose(tiles, (0, 2, 1, 3)).reshape(3, dec_seq * N_HEADS, PAGE_SIZE).astype(F32)


def _from_pairs(y_pm):
    return jnp.transpose(y_pm, (1, 0, 2)).reshape(y_pm.shape[1], HHD)


def _to_pairs(y):
    return jnp.transpose(y.reshape(y.shape[0], N_PAIRS, LANES), (1, 0, 2))


def _block_diag_rows(q, n_seq, dec_seq):
    q5 = q.reshape(n_seq, dec_seq, 1, N_HEADS, HEAD_DIM)
    eye = jnp.eye(N_HEADS, dtype=q.dtype).reshape(1, 1, N_HEADS, N_HEADS, 1)
    return (q5 * eye).reshape(n_seq, dec_seq * N_HEADS, HHD)


def _diag_heads(o_bd, n_seq, dec_seq):
    o6 = o_bd.reshape(n_seq, dec_seq, N_HEADS, N_HEADS, HEAD_DIM)
    hh = jnp.arange(N_HEADS)
    return o6[:, :, hh, hh, :].reshape(n_seq * dec_seq, HHD)


def _new_chunk_t(y, n_seq, dec_seq):
    y3 = jnp.transpose(y.reshape(n_seq, dec_seq, y.shape[-1]), (0, 2, 1))
    return jnp.pad(y3, ((0, 0), (0, 0), (0, PAGE_SIZE - dec_seq)))


def kernel(x_prompt, x_sample, cache_k, cache_v, cache_kidx, page_table, rel_table,
           norm_ffn1, ffn1_w_gu, ffn1_w_down, norm_mix, norm_ffn2, ffn2_w_gu, ffn2_w_down,
           a_w_in, a_w_out, a_q_norm, a_k_norm, a_kidx_norm,
           b_w_in, b_w_out, b_q_norm, b_k_norm, c_w_in, c_w_out):
    batch, seq, _ = x_prompt.shape
    n_seq, dec_seq, _ = x_sample.shape
    depth = cache_k.shape[0]
    n_pool = cache_k.shape[1]
    n_pages = page_table.shape[1]
    assert seq % TOKEN_TILE == 0 and QB == CH == MOBA_BLOCK and seq % QB == 0
    assert dec_seq * N_HEADS == N_ROWS and n_pages % PAGES_PER_STEP == 0
    assert PAGES_PER_STEP % PAGES_PER_BLOCK == 0 and cache_k.shape[2] == PAGE_SIZE

    xp = x_prompt.reshape(batch * seq, D_MODEL)
    xs = x_sample.reshape(n_seq * dec_seq, D_MODEL)
    cache_kt = jnp.transpose(cache_k, (0, 1, 3, 4, 2)).reshape(depth, n_pool, HHD, PAGE_SIZE)
    cache_vt = jnp.transpose(cache_v, (0, 1, 3, 4, 2)).reshape(depth, n_pool, HHD, PAGE_SIZE)
    cache_kit = jnp.transpose(cache_kidx, (0, 1, 3, 2))
    bias_p = _prompt_bias_tiles(rel_table)
    bias_s = _sample_bias_tiles(rel_table, dec_seq)
    n_blk = seq // MOBA_BLOCK

    new_k_p, new_v_p, new_ki_p, new_k_s, new_v_s, new_ki_s = [], [], [], [], [], []
    counts = [0, 0, 0]
    for i in range(depth):
        kind = i % 3
        slot = counts[kind]
        counts[kind] += 1
        wgu1, wd1 = ffn1_w_gu[i].astype(BF16), ffn1_w_down[i].astype(BF16)
        wgu2, wd2 = ffn2_w_gu[i].astype(BF16), ffn2_w_down[i].astype(BF16)
        xp = _ffn(xp, norm_ffn1[i], wgu1, wd1)
        xs = _ffn(xs, norm_ffn1[i], wgu1, wd1)

        dsa = None
        if kind == KIND_DSA:
            w_in = a_w_in[slot]
            w_main = w_in[:, :4 * HHD].astype(BF16)
            w_ki = w_in[:, 4 * HHD:4 * HHD + D_IDX]
            w_wi = jnp.pad(w_in[:, 4 * HHD + D_IDX:], ((0, 0), (0, LANES - N_IDX_HEADS)))
            w_tail = jnp.concatenate([w_ki, w_ki, w_wi], axis=1).astype(BF16)
            kw = dict(qg=a_q_norm[slot], kg=a_k_norm[slot], w_tail=w_tail, kig=a_kidx_norm[slot])
            kp, vp, qp_pm, kp_pm, _, vtp_pm, qip_pm, kip, ki2p, wip = _proj(kind, xp, norm_mix[i], w_main, **kw)
            ks, vs, qs_pm, ks_pm, vs_pm, _, qis_pm, kis, ki2s, wis = _proj(kind, xs, norm_mix[i], w_main, **kw)
            mask = _dsa_mask(batch, seq, qip_pm, ki2p, wip.T)
            op_pm = _masked_flash(kind, batch, seq, qp_pm, kp_pm, vtp_pm, bias_p, mask)
            qi_r = _from_pairs(qis_pm).reshape(n_seq, N_ROWS, D_IDX)
            wib = jnp.broadcast_to(wis.reshape(n_seq, N_ROWS, 1), (n_seq, N_ROWS, PAGE_SIZE))
            dsa = (qi_r, wib, cache_kit, slot, _new_chunk_t(ki2s[:, :D_IDX], n_seq, dec_seq))
            w_out = a_w_out[slot]
            new_ki_p.append(kip.reshape(batch, seq, D_IDX))
            new_ki_s.append(kis.reshape(n_seq, dec_seq, D_IDX))
        elif kind == KIND_MOBA:
            kw = dict(qg=b_q_norm[slot], kg=b_k_norm[slot])
            w_in = b_w_in[slot].astype(BF16)
            kp, vp, qp_pm, kp_pm, _, vtp_pm, kmean = _proj(kind, xp, norm_mix[i], w_in, with_kmean=True, **kw)
            ks, vs, qs_pm, ks_pm, vs_pm, _ = _proj(kind, xs, norm_mix[i], w_in, **kw)
            km = jnp.transpose(kmean.reshape(batch, n_blk, N_PAIRS, LANES), (2, 0, 1, 3))
            km = jnp.pad(km, ((0, 0), (0, 0), (0, LANES - n_blk), (0, 0))).astype(BF16)
            op_pm = _masked_flash(kind, batch, seq, qp_pm, kp_pm, vtp_pm, bias_p, km)
            w_out = b_w_out[slot]
        else:
            w_in = c_w_in[slot].astype(BF16)
            kp, vp, qp_pm, kp_pm, _, vtp_pm = _proj(kind, xp, norm_mix[i], w_in)
            ks, vs, qs_pm, ks_pm, vs_pm, _ = _proj(kind, xs, norm_mix[i], w_in)
            op_pm = _sb_flash(batch, seq, qp_pm, kp_pm, vtp_pm)
            w_out = c_w_out[slot]

        qbd = _block_diag_rows(_from_pairs(qs_pm), n_seq, dec_seq)
        knew_t = _new_chunk_t(_from_pairs(ks_pm), n_seq, dec_seq)
        vnew_t = _new_chunk_t(_from_pairs(vs_pm), n_seq, dec_seq)
        o_bd = _sample_attn(kind, i, page_table, cache_kt, cache_vt, qbd, knew_t, vnew_t,
                            bias=bias_s, dsa=dsa)
        os_pm = _to_pairs(_diag_heads(o_bd, n_seq, dec_seq)).astype(BF16)

        w_out = w_out.astype(BF16)
        xp = _outproj(xp, op_pm, w_out)
        xs = _outproj(xs, os_pm, w_out)
        xp = _ffn(xp, norm_ffn2[i], wgu2, wd2)
        xs = _ffn(xs, norm_ffn2[i], wgu2, wd2)

        new_k_p.append(kp.reshape(batch, seq, N_HEADS, HEAD_DIM))
        new_v_p.append(vp.reshape(batch, seq, N_HEADS, HEAD_DIM))
        new_k_s.append(ks.reshape(n_seq, dec_seq, N_HEADS, HEAD_DIM))
        new_v_s.append(vs.reshape(n_seq, dec_seq, N_HEADS, HEAD_DIM))

    return (xp.reshape(batch, seq, D_MODEL), xs.reshape(n_seq, dec_seq, D_MODEL),
            jnp.stack(new_k_p), jnp.stack(new_v_p), jnp.stack(new_ki_p),
            jnp.stack(new_k_s), jnp.stack(new_v_s), jnp.stack(new_ki_s))
```
